```python
import math
import jax
import jax.numpy as jnp
from jax import lax
import numpy as np

D_MODEL = 4096
BATCH = 32
SEQ = 256
DEPTH = 2
DEC_BATCH = 2
DEC_SEQ = 1024
PAST_LEN = 512

GRID_W = 64
N_DIR = 2
CONV_W = 4
CONV_PAD = (2, 1)
LRU_WIDTH = D_MODEL // 2
LRU_BLOCKS = 16
LRU_BLOCK = LRU_WIDTH // LRU_BLOCKS
LRU_C = 8.0
SSD_INNER = D_MODEL // 2
SSD_HEAD_DIM = 64
SSD_HEADS = SSD_INNER // SSD_HEAD_DIM
SSD_GROUPS = 4
SSD_STATE = 128
SSD_CHUNK = 128
SSD_BC = SSD_GROUPS * SSD_STATE
SSD_CONV_CH = SSD_INNER + 2 * SSD_BC
D_MIX = LRU_WIDTH + SSD_INNER
D_IN = 2 * LRU_WIDTH + SSD_INNER + SSD_CONV_CH + SSD_HEADS
IN_SPLITS = (LRU_WIDTH, 2 * LRU_WIDTH, 2 * LRU_WIDTH + SSD_INNER, 2 * LRU_WIDTH + SSD_INNER + SSD_CONV_CH)
N_EXPERTS = 32
TOP_K = 4
D_EXPERT = D_MODEL // 2
SWIGLU_LIMIT = 7.0
SWIGLU_ALPHA = 1.702
EPS = 1e-6
F32 = jnp.float32

kernel_name = 'hybrid_rglru_ssd_moe_flow_step'


def rmsnorm(x, g):
    xf = x.astype(F32)
    y = xf * lax.rsqrt(jnp.mean(xf * xf, axis=-1, keepdims=True) + EPS)
    return (y * g.astype(F32)).astype(x.dtype)


def group_rmsnorm(y, g):
    shp = y.shape
    yg = y.astype(F32).reshape(shp[:-1] + (SSD_GROUPS, shp[-1] // SSD_GROUPS))
    yg = yg * lax.rsqrt(jnp.mean(yg * yg, axis=-1, keepdims=True) + EPS)
    return (yg.reshape(shp) * g.astype(F32)).astype(y.dtype)


def flip_t(a):
    return jnp.flip(a, axis=1)


def dwconv(x, w, b):
    y = lax.conv_general_dilated(x, w[:, None, :], window_strides=(1,), padding=(CONV_PAD,),
                                 dimension_numbers=('NWC', 'WIO', 'NWC'),
                                 feature_group_count=x.shape[-1])
    return y + b


def linear_scan(a, b, h0):
    b = b.at[:, 0].add(a[:, 0] * h0)

    def combine(left, right):
        return left[0] * right[0], right[0] * left[1] + right[1]

    _, h = lax.associative_scan(combine, (a, b), axis=1)
    return h


def rglru_scan(xc, h0, wa, ba, wi, bi, lam):
    bn, t, w = xc.shape
    xb = xc.reshape(bn, t, LRU_BLOCKS, LRU_BLOCK)
    r = jax.nn.sigmoid(jnp.einsum('btnk,nkj->btnj', xb, wa).reshape(bn, t, w) + ba)
    i = jax.nn.sigmoid(jnp.einsum('btnk,nkj->btnj', xb, wi).reshape(bn, t, w) + bi)
    log_a = -LRU_C * r.astype(F32) * jax.nn.softplus(-lam.astype(F32))
    a = jnp.exp(log_a)
    b = jnp.sqrt(-jnp.expm1(2.0 * log_a)) * (i * xc).astype(F32)
    h = linear_scan(a, b, h0.astype(F32))
    return h.astype(xc.dtype), h[:, -1].astype(xc.dtype)


def segsum(x):
    t = x.shape[-1]
    xe = jnp.broadcast_to(x[..., :, None], x.shape + (t,))
    xe = jnp.where(jnp.tril(jnp.ones((t, t), dtype=bool), -1), xe, 0.0)
    cs = jnp.cumsum(xe, axis=-2)
    return jnp.where(jnp.tril(jnp.ones((t, t), dtype=bool)), cs, -jnp.inf)


def ssd_chunked(x, dt, a, bm, cm, s0):
    bn, t, h, p = x.shape
    g, n = bm.shape[-2:]
    r = h // g
    nc = t // SSD_CHUNK
    xd = (x * dt[..., None]).reshape(bn, nc, SSD_CHUNK, g, r, p)
    da = jnp.moveaxis((dt * a).reshape(bn, nc, SSD_CHUNK, g, r), 2, -1)
    bc = bm.reshape(bn, nc, SSD_CHUNK, g, n)
    cc = cm.reshape(bn, nc, SSD_CHUNK, g, n)
    cum = jnp.cumsum(da, axis=-1)
    scores = jnp.einsum('bcign,bcjgn->bcgij', cc, bc)
    y_diag = jnp.einsum('bcgij,bcgrij,bcjgrp->bcigrp', scores, jnp.exp(segsum(da)), xd)
    decay_st = jnp.exp(cum[..., -1:] - cum)
    states = jnp.einsum('bcjgn,bcgrj,bcjgrp->bcgrpn', bc, decay_st, xd)
    states = jnp.concatenate([s0.reshape(bn, 1, g, r, p, n), states], axis=1)
    last = jnp.pad(jnp.moveaxis(cum[..., -1], 1, -1), ((0, 0), (0, 0), (0, 0), (1, 0)))
    new_states = jnp.einsum('bgrzc,bcgrpn->bzgrpn', jnp.exp(segsum(last)), states)
    states_in, s_final = new_states[:, :-1], new_states[:, -1]
    y_off = jnp.einsum('bcign,bcgrpn,bcgri->bcigrp', cc, states_in, jnp.exp(cum))
    y = (y_diag + y_off).reshape(bn, t, h, p)
    return y, s_final.reshape(bn, h, p, n)


def ssd_direction(xs, bm, cm, dt_raw, s0, dt_bias, a_log, d_skip):
    dt = jax.nn.softplus(dt_raw.astype(F32) + dt_bias.astype(F32))
    a = -jnp.exp(a_log.astype(F32))
    y, s_fin = ssd_chunked(xs.astype(F32), dt, a, bm.astype(F32), cm.astype(F32), s0.astype(F32))
    y = y + d_skip.astype(F32)[:, None] * xs.astype(F32)
    return y.astype(xs.dtype), s_fin.astype(xs.dtype)


def mixer(h, lru_h0, ssd_s0, P, l):
    bn, t, _ = h.shape
    proj = h @ P['w_in'][l]
    lru_x, lru_g, z, xbc, dt_raw = jnp.split(proj, IN_SPLITS, axis=-1)
    xc = dwconv(lru_x, P['lru_conv_w'][l], P['lru_conv_b'][l])
    lf, hf = rglru_scan(xc, lru_h0[:, 0], P['lru_wa'][l, 0], P['lru_ba'][l, 0],
                        P['lru_wi'][l, 0], P['lru_bi'][l, 0], P['lru_lambda'][l, 0])
    lb, hb = rglru_scan(flip_t(xc), lru_h0[:, 1], P['lru_wa'][l, 1], P['lru_ba'][l, 1],
                        P['lru_wi'][l, 1], P['lru_bi'][l, 1], P['lru_lambda'][l, 1])
    y_lru = (lf + flip_t(lb)) * jax.nn.gelu(lru_g)
    xbc = jax.nn.silu(dwconv(xbc, P['ssd_conv_w'][l], P['ssd_conv_b'][l]))
    xs, bm, cm = jnp.split(xbc, (SSD_INNER, SSD_INNER + SSD_BC), axis=-1)
    xs = xs.reshape(bn, t, SSD_HEADS, SSD_HEAD_DIM)
    bm = bm.reshape(bn, t, SSD_GROUPS, SSD_STATE)
    cm = cm.reshape(bn, t, SSD_GROUPS, SSD_STATE)
    sfw, sf = ssd_direction(xs, bm, cm, dt_raw, ssd_s0[:, 0], P['ssd_dt_bias'][l, 0],
                            P['ssd_a_log'][l, 0], P['ssd_d'][l, 0])
    sbw, sb = ssd_direction(flip_t(xs), flip_t(bm), flip_t(cm), flip_t(dt_raw), ssd_s0[:, 1],
                            P['ssd_dt_bias'][l, 1], P['ssd_a_log'][l, 1], P['ssd_d'][l, 1])
    y_ssd = (sfw + flip_t(sbw)).reshape(bn, t, SSD_INNER)
    y_ssd = group_rmsnorm(y_ssd * jax.nn.silu(z), P['ssd_norm_g'][l])
    out = jnp.concatenate([y_lru, y_ssd], axis=-1) @ P['w_out'][l]
    return out, (hf, hb, sf, sb)


def moe(h, P, l):
    bn, t, d = h.shape
    tok = h.reshape(bn * t, d)
    logits = (tok @ P['router_w'][l] + P['router_b'][l]).astype(F32)
    top_v, top_i = lax.top_k(logits, TOP_K)
    w = jax.nn.softmax(top_v, axis=-1)
    combine = jnp.sum(jax.nn.one_hot(top_i, N_EXPERTS, dtype=F32) * w[..., None], axis=1).astype(h.dtype)
    out = jnp.zeros_like(tok)
    for e in range(N_EXPERTS):
        g = jnp.minimum(tok @ P['exp_w_gate'][l, e] + P['exp_b_gate'][l, e], SWIGLU_LIMIT)
        u = jnp.clip(tok @ P['exp_w_up'][l, e] + P['exp_b_up'][l, e], -SWIGLU_LIMIT, SWIGLU_LIMIT)
        y = ((u + 1.0) * g * jax.nn.sigmoid(SWIGLU_ALPHA * g)) @ P['exp_w_down'][l, e] + P['exp_b_down'][l, e]
        out = out + combine[:, e:e + 1] * y
    return out.reshape(bn, t, d)


def trunk(x, cond, lru0, ssd0, P, collect_state):
    lru_states, ssd_states = [], []
    for l in range(DEPTH):
        mod = (jax.nn.silu(cond) @ P['ada_w'][l] + P['ada_b'][l])[:, None, :]
        sh1, sc1, g1, sh2, sc2, g2 = jnp.split(mod, 6, axis=-1)
        h = rmsnorm(x, P['norm1_g'][l]) * (1.0 + sc1) + sh1
        m, (hf, hb, sf, sb) = mixer(h, lru0[:, l], ssd0[:, l], P, l)
        x = x + g1 * m
        h = rmsnorm(x, P['norm2_g'][l]) * (1.0 + sc2) + sh2
        x = x + g2 * moe(h, P, l)
        if collect_state:
            lru_states.append(jnp.stack([hf, hb], axis=1))
            ssd_states.append(jnp.stack([sf, sb], axis=1))
    y = rmsnorm(x, P['final_norm_g'])
    if collect_state:
        return y, jnp.stack(lru_states, axis=1), jnp.stack(ssd_states, axis=1)
    return y


def setup_inputs(seed: int = 0) -> dict:
    key = jax.random.key(seed)
    ks = iter(jax.random.split(key, 48))

    def nrm(shape, scale):
        return jax.random.normal(next(ks), shape, F32) * scale

    u_lam = jax.random.uniform(next(ks), (DEPTH, N_DIR, LRU_WIDTH), F32, 0.9, 0.999) ** (1.0 / LRU_C)
    dt0 = jnp.exp(jax.random.uniform(next(ks), (DEPTH, N_DIR, SSD_HEADS), F32, math.log(1e-3), math.log(1e-1)))
    a0 = jax.random.uniform(next(ks), (DEPTH, N_DIR, SSD_HEADS), F32, 1.0, 16.0)
    return {
        'x_prompt': nrm((BATCH, SEQ, D_MODEL), 1.0),
        'x_sample': nrm((DEC_BATCH, DEC_SEQ, D_MODEL), 1.0),
        'c': nrm((DEC_BATCH, D_MODEL), 1.0),
        'state_lru': nrm((DEC_BATCH, DEPTH, N_DIR, LRU_WIDTH), 0.5),
        'state_ssd': nrm((DEC_BATCH, DEPTH, N_DIR, SSD_HEADS, SSD_HEAD_DIM, SSD_STATE), 0.5),
        'c_ctx': nrm((D_MODEL,), 1.0),
        'ada_w': nrm((DEPTH, D_MODEL, 6 * D_MODEL), 0.5 * D_MODEL ** -0.5),
        'ada_b': nrm((DEPTH, 6 * D_MODEL), 0.02),
        'norm1_g': 1.0 + nrm((DEPTH, D_MODEL), 0.1),
        'norm2_g': 1.0 + nrm((DEPTH, D_MODEL), 0.1),
        'w_in': nrm((DEPTH, D_MODEL, D_IN), D_MODEL ** -0.5),
        'lru_conv_w': nrm((DEPTH, CONV_W, LRU_WIDTH), CONV_W ** -0.5),
        'lru_conv_b': nrm((DEPTH, LRU_WIDTH), 0.02),
        'lru_wa': nrm((DEPTH, N_DIR, LRU_BLOCKS, LRU_BLOCK, LRU_BLOCK), LRU_BLOCK ** -0.5),
        'lru_ba': nrm((DEPTH, N_DIR, LRU_WIDTH), 0.02),
        'lru_wi': nrm((DEPTH, N_DIR, LRU_BLOCKS, LRU_BLOCK, LRU_BLOCK), LRU_BLOCK ** -0.5),
        'lru_bi': nrm((DEPTH, N_DIR, LRU_WIDTH), 0.02),
        'lru_lambda': jnp.log(u_lam) - jnp.log1p(-u_lam),
        'ssd_conv_w': nrm((DEPTH, CONV_W, SSD_CONV_CH), CONV_W ** -0.5),
        'ssd_conv_b': nrm((DEPTH, SSD_CONV_CH), 0.02),
        'ssd_dt_bias': dt0 + jnp.log(-jnp.expm1(-dt0)),
        'ssd_a_log': jnp.log(a0),
        'ssd_d': 1.0 + nrm((DEPTH, N_DIR, SSD_HEADS), 0.1),
        'ssd_norm_g': 1.0 + nrm((DEPTH, SSD_INNER), 0.1),
        'w_out': nrm((DEPTH, D_MIX, D_MODEL), D_MIX ** -0.5),
        'router_w': nrm((DEPTH, D_MODEL, N_EXPERTS), D_MODEL ** -0.5),
        'router_b': nrm((DEPTH, N_EXPERTS), 0.01),
        'exp_w_gate': nrm((DEPTH, N_EXPERTS, D_MODEL, D_EXPERT), D_MODEL ** -0.5),
        'exp_b_gate': nrm((DEPTH, N_EXPERTS, D_EXPERT), 0.02),
        'exp_w_up': nrm((DEPTH, N_EXPERTS, D_MODEL, D_EXPERT), D_MODEL ** -0.5),
        'exp_b_up': nrm((DEPTH, N_EXPERTS, D_EXPERT), 0.02),
        'exp_w_down': nrm((DEPTH, N_EXPERTS, D_EXPERT, D_MODEL), D_EXPERT ** -0.5),
        'exp_b_down': nrm((DEPTH, N_EXPERTS, D_MODEL), 0.02),
        'final_norm_g': 1.0 + nrm((D_MODEL,), 0.1),
    }


def reference(x_prompt, x_sample, c, state_lru, state_ssd, c_ctx, ada_w, ada_b, norm1_g, norm2_g,
              w_in, lru_conv_w, lru_conv_b, lru_wa, lru_ba, lru_wi, lru_bi, lru_lambda,
              ssd_conv_w, ssd_conv_b, ssd_dt_bias, ssd_a_log, ssd_d, ssd_norm_g, w_out,
              router_w, router_b, exp_w_gate, exp_b_gate, exp_w_up, exp_b_up, exp_w_down, exp_b_down,
              final_norm_g):
    P = {
        'ada_w': ada_w, 'ada_b': ada_b, 'norm1_g': norm1_g, 'norm2_g': norm2_g, 'w_in': w_in,
        'lru_conv_w': lru_conv_w, 'lru_conv_b': lru_conv_b, 'lru_wa': lru_wa, 'lru_ba': lru_ba,
        'lru_wi': lru_wi, 'lru_bi': lru_bi, 'lru_lambda': lru_lambda,
        'ssd_conv_w': ssd_conv_w, 'ssd_conv_b': ssd_conv_b, 'ssd_dt_bias': ssd_dt_bias,
        'ssd_a_log': ssd_a_log, 'ssd_d': ssd_d, 'ssd_norm_g': ssd_norm_g, 'w_out': w_out,
        'router_w': router_w, 'router_b': router_b, 'exp_w_gate': exp_w_gate, 'exp_b_gate': exp_b_gate,
        'exp_w_up': exp_w_up, 'exp_b_up': exp_b_up, 'exp_w_down': exp_w_down, 'exp_b_down': exp_b_down,
        'final_norm_g': final_norm_g,
    }
    b_ctx = x_prompt.shape[0]
    lru0 = jnp.zeros((b_ctx, DEPTH, N_DIR, LRU_WIDTH), x_prompt.dtype)
    ssd0 = jnp.zeros((b_ctx, DEPTH, N_DIR, SSD_HEADS, SSD_HEAD_DIM, SSD_STATE), x_prompt.dtype)
    y_prompt, new_state_lru, new_state_ssd = trunk(x_prompt, c_ctx[None, :], lru0, ssd0, P, True)
    y_sample = trunk(x_sample, c, state_lru, state_ssd, P, False)
    return (y_prompt, y_sample, new_state_lru, new_state_ssd)
```

```python
import functools

import jax
import jax.numpy as jnp
from jax import lax
from jax.experimental import pallas as pl
from jax.experimental.pallas import tpu as pltpu

F32 = jnp.float32
BF16 = jnp.bfloat16
I32 = jnp.int32
HIGHEST = lax.Precision.HIGHEST

TOP_K = 4
LRU_C = 8.0
SWIGLU_LIMIT = 7.0
SWIGLU_ALPHA = 1.702
EPS = 1e-6
SSD_CHUNK = 128
LANES = 128
NEG_BIG = -1e30
VMEM_LIMIT_BYTES = 56 * 1024 * 1024
MOE_TILE = 256
COMBINE_TILE = 128

SDS = jax.ShapeDtypeStruct


def _params(*sem):
    return pltpu.CompilerParams(dimension_semantics=sem, vmem_limit_bytes=VMEM_LIMIT_BYTES)


def _dot(a, b):
    return jnp.dot(a, b, preferred_element_type=F32)


def _dot_exact(a, b):
    return jnp.dot(a, b, precision=HIGHEST, preferred_element_type=F32)


def _softplus(x):
    return jnp.maximum(x, 0.0) + jnp.log1p(jnp.exp(-jnp.abs(x)))


def _silu(x):
    return x * jax.nn.sigmoid(x)


def _cond_row(i, tm, n_ctx, t_lat):
    r = i * tm
    return jnp.where(r < n_ctx, 0, 1 + (r - n_ctx) // t_lat)


def _ada_kernel(c_ref, w_ref, b_ref, o_ref):
    c = c_ref[...]
    o_ref[...] = _dot_exact(_silu(c), w_ref[...]) + b_ref[...]


def _ada_mod(cond8, ada_w, ada_b):
    depth, d, d6 = ada_w.shape
    tn = min(512, d6)
    return pl.pallas_call(
        _ada_kernel,
        out_shape=SDS((depth, 8, d6), F32),
        grid=(depth, d6 // tn),
        in_specs=[
            pl.BlockSpec((8, d), lambda l, j: (0, 0)),
            pl.BlockSpec((None, d, tn), lambda l, j: (l, 0, j)),
            pl.BlockSpec((None, 1, tn), lambda l, j: (l, 0, j)),
        ],
        out_specs=pl.BlockSpec((None, 8, tn), lambda l, j: (l, 0, j)),
        compiler_params=_params("arbitrary", "arbitrary"),
        name="ada_mod",
    )(cond8, ada_w, ada_b.reshape(depth, 1, d6))


def _norm_mod(x, g, sc, sh):
    ms = jnp.mean(x * x, axis=-1, keepdims=True)
    return (x * lax.rsqrt(ms + EPS) * g) * (1.0 + sc) + sh


def _normmod_kernel(x_ref, g_ref, sc_ref, sh_ref, o_ref):
    o_ref[...] = _norm_mod(x_ref[...], g_ref[...], sc_ref[...], sh_ref[...]).astype(o_ref.dtype)


def _normmod(x, g, modr, sh_idx, sc_idx, cond_of_tile, tm):
    n, d = x.shape
    return pl.pallas_call(
        _normmod_kernel,
        out_shape=SDS((n, d), BF16),
        grid=(n // tm,),
        in_specs=[
            pl.BlockSpec((tm, d), lambda i: (i, 0)),
            pl.BlockSpec((1, d), lambda i: (0, 0)),
            pl.BlockSpec((None, None, 1, d), lambda i: (cond_of_tile(i, tm), sc_idx, 0, 0)),
            pl.BlockSpec((None, None, 1, d), lambda i: (cond_of_tile(i, tm), sh_idx, 0, 0)),
        ],
        out_specs=pl.BlockSpec((tm, d), lambda i: (i, 0)),
        compiler_params=_params("arbitrary"),
        name="norm_mod",
    )(x, g.reshape(1, d), modr, modr)


def _mm_kernel(a_ref, w_ref, o_ref):
    o_ref[...] = _dot(a_ref[...], w_ref[...]).astype(o_ref.dtype)


def _matmul(a, w, tm, tn, out_dtype=F32):
    n, k = a.shape
    m = w.shape[1]
    return pl.pallas_call(
        _mm_kernel,
        out_shape=SDS((n, m), out_dtype),
        grid=(m // tn, n // tm),
        in_specs=[
            pl.BlockSpec((tm, k), lambda j, i: (i, 0)),
            pl.BlockSpec((k, tn), lambda j, i: (0, j)),
        ],
        out_specs=pl.BlockSpec((tm, tn), lambda j, i: (i, j)),
        compiler_params=_params("arbitrary", "arbitrary"),
        name="matmul",
    )(a, w)


def _wout_kernel(a1_ref, a2_ref, w1_ref, w2_ref, x_ref, g_ref, o_ref):
    m = _dot(a1_ref[...], w1_ref[...]) + _dot(a2_ref[...], w2_ref[...])
    o_ref[...] = x_ref[...] + g_ref[...] * m


def _wout(y1, y2, w, x, modr, g_idx, cond_of_tile, tm, tn):
    n, k1 = y1.shape
    k2 = y2.shape[1]
    d = w.shape[1]
    assert k1 == k2
    return pl.pallas_call(
        _wout_kernel,
        out_shape=SDS((n, d), F32),
        grid=(d // tn, n // tm),
        in_specs=[
            pl.BlockSpec((tm, k1), lambda j, i: (i, 0)),
            pl.BlockSpec((tm, k2), lambda j, i: (i, 0)),
            pl.BlockSpec((k1, tn), lambda j, i: (0, j)),
            pl.BlockSpec((k2, tn), lambda j, i: (1, j)),
            pl.BlockSpec((tm, tn), lambda j, i: (i, j)),
            pl.BlockSpec((None, None, 1, tn), lambda j, i: (cond_of_tile(i, tm), g_idx, 0, j)),
        ],
        out_specs=pl.BlockSpec((tm, tn), lambda j, i: (i, j)),
        compiler_params=_params("arbitrary", "arbitrary"),
        name="w_out",
    )(y1, y2, w, w, x, modr)


def _shift_rows(x, s):
    t_len = x.shape[0]
    rolled = pltpu.roll(x, s % t_len, axis=0)
    t = lax.broadcasted_iota(I32, x.shape, 0)
    keep = (t >= s) if s > 0 else (t < t_len + s)
    return jnp.where(keep, rolled, 0.0)


def _dwconv(x, w_ref, b_ref):
    return (b_ref[...] + w_ref[2:3, :] * x + w_ref[1:2, :] * _shift_rows(x, 1)
            + w_ref[0:1, :] * _shift_rows(x, 2) + w_ref[3:4, :] * _shift_rows(x, -1))


def _lru_kernel(x_ref, g_ref, cw_ref, cb_ref, wg_ref, gb_ref, lam_ref, h0_ref,
                y_ref, hfin_ref, af_s, bf_s, ab_s, bb_s):
    t_len, tc = x_ref.shape
    xc = _dwconv(x_ref[...], cw_ref, cb_ref)
    for n in range(tc // LANES):
        sl = slice(n * LANES, (n + 1) * LANES)
        xn = xc[:, sl]
        gates = _dot(xn.astype(BF16), wg_ref[n])
        for d, (a_s, b_s) in enumerate(((af_s, bf_s), (ab_s, bb_s))):
            r = jax.nn.sigmoid(gates[:, (2 * d) * LANES:(2 * d + 1) * LANES] + gb_ref[2 * d:2 * d + 1, sl])
            i = jax.nn.sigmoid(gates[:, (2 * d + 1) * LANES:(2 * d + 2) * LANES]
                               + gb_ref[2 * d + 1:2 * d + 2, sl])
            log_a = -LRU_C * r * _softplus(-lam_ref[d:d + 1, sl])
            a_s[:, sl] = jnp.exp(log_a)
            th = jnp.tanh(log_a)
            b_s[:, sl] = jnp.sqrt(-2.0 * th / (1.0 - th)) * (i * xn)

    def step(s, carry):
        hf, hb = carry
        hf = af_s[pl.ds(s, 1), :] * hf + bf_s[pl.ds(s, 1), :]
        bf_s[pl.ds(s, 1), :] = hf
        u = t_len - 1 - s
        hb = ab_s[pl.ds(u, 1), :] * hb + bb_s[pl.ds(u, 1), :]
        bb_s[pl.ds(u, 1), :] = hb
        return hf, hb

    hf, hb = lax.fori_loop(0, t_len, step, (h0_ref[0:1, :], h0_ref[1:2, :]), unroll=8)
    hfin_ref[0:1, :] = hf
    hfin_ref[1:2, :] = hb
    y_ref[...] = ((bf_s[...] + bb_s[...]) * jax.nn.gelu(g_ref[...], approximate=True)).astype(y_ref.dtype)


def _lru_group(proj, row_off, nb, t_len, h0, cw, cb, wg, gb, lam):
    w = cw.shape[1]
    tc = min(512, w)
    rb = row_off // t_len
    nct = w // tc
    return pl.pallas_call(
        _lru_kernel,
        out_shape=(SDS((nb * t_len, w), BF16), SDS((nb, 2, w), F32)),
        grid=(nb, nct),
        in_specs=[
            pl.BlockSpec((t_len, tc), lambda b, j: (rb + b, j)),
            pl.BlockSpec((t_len, tc), lambda b, j: (rb + b, nct + j)),
            pl.BlockSpec((4, tc), lambda b, j: (0, j)),
            pl.BlockSpec((1, tc), lambda b, j: (0, j)),
            pl.BlockSpec((tc // LANES, LANES, 4 * LANES), lambda b, j: (j, 0, 0)),
            pl.BlockSpec((4, tc), lambda b, j: (0, j)),
            pl.BlockSpec((2, tc), lambda b, j: (0, j)),
            pl.BlockSpec((None, 2, tc), lambda b, j: (b, 0, j)),
        ],
        out_specs=(
            pl.BlockSpec((t_len, tc), lambda b, j: (b, j)),
            pl.BlockSpec((None, 2, tc), lambda b, j: (b, 0, j)),
        ),
        scratch_shapes=[pltpu.VMEM((t_len, tc), F32)] * 4,
        compiler_params=_params("arbitrary", "arbitrary"),
        name="rglru",
    )(proj, proj, cw, cb, wg, gb, lam, h0)


def _ssd_kernel(xs_ref, bm_ref, cm_ref, z_ref, dt_ref, dtt_ref,
                cwx_ref, cwb_ref, cwc_ref, cbx_ref, cbb_ref, cbc_ref,
                dtb_ref, dtbt_ref, alog_ref, alogt_ref, dsum_ref, ng_ref, s0_ref,
                y_ref, sfin_ref, xs_s, bm_s, cm_s, xd_s, dt_s, yacc_s, st_s):
    t_len, gw = xs_ref.shape
    q = SSD_CHUNK
    nc = t_len // q
    hg = dtt_ref.shape[1]
    hp = gw // hg
    per_blk = LANES // hp

    xs_s[...] = _silu(_dwconv(xs_ref[...], cwx_ref, cbx_ref))
    bm_s[...] = _silu(_dwconv(bm_ref[...], cwb_ref, cbb_ref)).astype(BF16)
    cm_s[...] = _silu(_dwconv(cm_ref[...], cwc_ref, cbc_ref)).astype(BF16)

    row = lax.broadcasted_iota(I32, (q, q), 0)
    col = lax.broadcasted_iota(I32, (q, q), 1)
    e_h = lax.broadcasted_iota(I32, (LANES, gw), 0)
    e_c = lax.broadcasted_iota(I32, (LANES, gw), 1)
    expand = (e_c // hp == e_h).astype(F32)
    lane_in_blk = lax.broadcasted_iota(I32, (q, LANES), 1) // hp

    for d in range(2):
        causal = (row >= col) if d == 0 else (row <= col)
        tri = causal.astype(F32)
        tri_t = ((col >= row) if d == 0 else (col <= row)).astype(F32)
        last = q - 1 if d == 0 else 0
        a_row = -jnp.exp(alog_ref[d])
        a_col = -jnp.exp(alogt_ref[d])
        dt_all = _softplus(dt_ref[...] + dtb_ref[d])
        dt_s[...] = dt_all
        xd_s[...] = xs_s[...] * _dot_exact(dt_all, expand)
        st_s[...] = s0_ref[d].T

        def chunk(ci, carry):
            c = ci if d == 0 else nc - 1 - ci
            c0 = pl.multiple_of(c * q, q)
            rows = pl.ds(c0, q)
            da = dt_s[rows, :] * a_row
            da_t = _softplus(dtt_ref[c] + dtbt_ref[d]) * a_col
            cum = _dot_exact(tri, da)
            cum_t = _dot_exact(da_t, tri_t)
            tot = cum[last:last + 1, :]
            expcum_x = _dot_exact(jnp.exp(cum), expand)
            decay_x = _dot_exact(jnp.exp(tot - cum), expand)
            tot_x = expcum_x[last:last + 1, :]
            bm = bm_s[rows, :]
            cm = cm_s[rows, :]
            xd = xd_s[rows, :]
            scores = lax.dot_general(cm, bm, (((1,), (1,)), ((), ())), preferred_element_type=F32)
            st = st_s[...]
            y_off = _dot(cm, st.astype(BF16)) * expcum_x
            for blk in range(gw // LANES):
                xd_blk = xd[:, blk * LANES:(blk + 1) * LANES]
                acc = None
                for sub in range(per_blk):
                    h = blk * per_blk + sub
                    diff = cum[:, h:h + 1] - cum_t[h:h + 1, :]
                    lmat = jnp.exp(jnp.where(causal, diff, NEG_BIG))
                    m = (scores * lmat).astype(BF16)
                    xh = jnp.where(lane_in_blk == sub, xd_blk, 0.0).astype(BF16)
                    part = _dot(m, xh)
                    acc = part if acc is None else acc + part
                sl = slice(blk * LANES, (blk + 1) * LANES)
                contrib = acc + y_off[:, sl]
                if d == 0:
                    yacc_s[rows, sl] = contrib
                else:
                    yacc_s[rows, sl] = yacc_s[rows, sl] + contrib
            upd = lax.dot_general(bm, (xd * decay_x).astype(BF16), (((0,), (0,)), ((), ())),
                                  preferred_element_type=F32)
            st_s[...] = st * tot_x + upd
            return carry

        lax.fori_loop(0, nc, chunk, 0)
        sfin_ref[d] = st_s[...].T

    z = z_ref[...]
    y = (yacc_s[...] + dsum_ref[...] * xs_s[...]) * _silu(z)
    ms = jnp.mean(y * y, axis=-1, keepdims=True)
    y_ref[...] = (y * lax.rsqrt(ms + EPS) * ng_ref[...]).astype(y_ref.dtype)


def _ssd_group(proj, dtg, dtgt, row_off, nb, t_len, s0, cw, cb, dtb, dtbt, alog, alogt, dsum, ng, dims):
    off_z, off_x, off_b, off_c, inner, groups, nstate = dims
    gw = inner // groups
    hg = dtgt.shape[2]
    q = SSD_CHUNK
    nc = t_len // q
    rb = row_off // t_len
    cx, cbk, cck = (off_x - off_x) // gw, (off_b - off_x) // nstate, (off_c - off_x) // nstate
    return pl.pallas_call(
        _ssd_kernel,
        out_shape=(SDS((nb * t_len, inner), BF16), SDS((nb, 2, groups, gw, nstate), F32)),
        grid=(nb, groups),
        in_specs=[
            pl.BlockSpec((t_len, gw), lambda b, g: (rb + b, off_x // gw + g)),
            pl.BlockSpec((t_len, nstate), lambda b, g: (rb + b, off_b // nstate + g)),
            pl.BlockSpec((t_len, nstate), lambda b, g: (rb + b, off_c // nstate + g)),
            pl.BlockSpec((t_len, gw), lambda b, g: (rb + b, off_z // gw + g)),
            pl.BlockSpec((None, t_len, LANES), lambda b, g: (g, rb + b, 0)),
            pl.BlockSpec((None, nc, hg, q), lambda b, g: (g, rb + b, 0, 0)),
            pl.BlockSpec((4, gw), lambda b, g: (0, cx + g)),
            pl.BlockSpec((4, nstate), lambda b, g: (0, cbk + g)),
            pl.BlockSpec((4, nstate), lambda b, g: (0, cck + g)),
            pl.BlockSpec((1, gw), lambda b, g: (0, cx + g)),
            pl.BlockSpec((1, nstate), lambda b, g: (0, cbk + g)),
            pl.BlockSpec((1, nstate), lambda b, g: (0, cck + g)),
            pl.BlockSpec((None, 2, 1, LANES), lambda b, g: (g, 0, 0, 0)),
            pl.BlockSpec((None, 2, hg, 1), lambda b, g: (g, 0, 0, 0)),
            pl.BlockSpec((None, 2, 1, LANES), lambda b, g: (g, 0, 0, 0)),
            pl.BlockSpec((None, 2, hg, 1), lambda b, g: (g, 0, 0, 0)),
            pl.BlockSpec((None, 1, gw), lambda b, g: (g, 0, 0)),
            pl.BlockSpec((1, gw), lambda b, g: (0, g)),
            pl.BlockSpec((None, 2, None, gw, nstate), lambda b, g: (b, 0, g, 0, 0)),
        ],
        out_specs=(
            pl.BlockSpec((t_len, gw), lambda b, g: (b, g)),
            pl.BlockSpec((None, 2, None, gw, nstate), lambda b, g: (b, 0, g, 0, 0)),
        ),
        scratch_shapes=[
            pltpu.VMEM((t_len, gw), F32),
            pltpu.VMEM((t_len, nstate), BF16),
            pltpu.VMEM((t_len, nstate), BF16),
            pltpu.VMEM((t_len, gw), F32),
            pltpu.VMEM((t_len, LANES), F32),
            pltpu.VMEM((t_len, gw), F32),
            pltpu.VMEM((nstate, gw), F32),
        ],
        compiler_params=_params("arbitrary", "arbitrary"),
        name="ssd",
    )(proj, proj, proj, proj, dtg, dtgt, cw, cw, cw, cb, cb, cb, dtb, dtbt, alog, alogt, dsum, ng, s0)


def _router_kernel(x_ref, g_ref, sc_ref, sh_ref, rw_ref, rb_ref,
                   h_ref, w_ref, i_ref, r_ref, cnt_ref, carry_s, *, n_experts):
    tm = x_ref.shape[0]

    @pl.when(pl.program_id(0) == 0)
    def _():
        carry_s[...] = jnp.zeros_like(carry_s)

    h = _norm_mod(x_ref[...], g_ref[...], sc_ref[...], sh_ref[...])
    h_ref[...] = h
    rw = rw_ref[...]
    rw_hi = rw.astype(BF16)
    rw_lo = (rw - rw_hi.astype(F32)).astype(BF16)
    h_hi = h.astype(BF16)
    h_lo = (h - h_hi.astype(F32)).astype(BF16)
    logits = _dot(h_hi, rw_hi) + _dot(h_lo, rw_hi) + _dot(h_hi, rw_lo) + rb_ref[...]
    lane = lax.broadcasted_iota(I32, (tm, LANES), 1)
    cur = jnp.where(lane < n_experts, logits, -jnp.inf)
    vals, idxs = [], []
    member = jnp.zeros((tm, LANES), F32)
    for _k in range(TOP_K):
        m = jnp.max(cur, axis=-1, keepdims=True)
        idx = jnp.min(jnp.where(cur == m, lane, LANES), axis=-1, keepdims=True)
        sel = lane == idx
        vals.append(m)
        idxs.append(idx)
        member = member + sel.astype(F32)
        cur = jnp.where(sel, -jnp.inf, cur)
    exps = [jnp.exp(v - vals[0]) for v in vals]
    denom = exps[0]
    for e in exps[1:]:
        denom = denom + e
    r_i = lax.broadcasted_iota(I32, (tm, tm), 0)
    c_i = lax.broadcasted_iota(I32, (tm, tm), 1)
    earlier = (c_i < r_i).astype(BF16)
    before = _dot(earlier, member.astype(BF16)) + carry_s[...]
    w_out = jnp.zeros((tm, LANES), F32)
    i_out = jnp.zeros((tm, LANES), I32)
    r_out = jnp.zeros((tm, LANES), I32)
    for k in range(TOP_K):
        rank = jnp.sum(jnp.where(lane == idxs[k], before, 0.0), axis=-1, keepdims=True)
        w_out = jnp.where(lane == k, exps[k] / denom, w_out)
        i_out = jnp.where(lane == k, idxs[k], i_out)
        r_out = jnp.where(lane == k, rank.astype(I32), r_out)
    w_ref[...] = w_out
    i_ref[...] = i_out
    r_ref[...] = r_out
    carry_s[...] = carry_s[...] + jnp.sum(member, axis=0, keepdims=True)
    cnt_ref[...] = jnp.broadcast_to(carry_s[...], cnt_ref.shape)


def _router(x, g, modr, sh_idx, sc_idx, cond_of_tile, rw_pad, rb_pad, n_experts, tm):
    n, d = x.shape
    return pl.pallas_call(
        functools.partial(_router_kernel, n_experts=n_experts),
        out_shape=(SDS((n, d), F32), SDS((n, LANES), F32), SDS((n, LANES), I32),
                   SDS((n, LANES), I32), SDS((8, LANES), F32)),
        grid=(n // tm,),
        in_specs=[
            pl.BlockSpec((tm, d), lambda i: (i, 0)),
            pl.BlockSpec((1, d), lambda i: (0, 0)),
            pl.BlockSpec((None, None, 1, d), lambda i: (cond_of_tile(i, tm), sc_idx, 0, 0)),
            pl.BlockSpec((None, None, 1, d), lambda i: (cond_of_tile(i, tm), sh_idx, 0, 0)),
            pl.BlockSpec((d, LANES), lambda i: (0, 0)),
            pl.BlockSpec((1, LANES), lambda i: (0, 0)),
        ],
        out_specs=(
            pl.BlockSpec((tm, d), lambda i: (i, 0)),
            pl.BlockSpec((tm, LANES), lambda i: (i, 0)),
            pl.BlockSpec((tm, LANES), lambda i: (i, 0)),
            pl.BlockSpec((tm, LANES), lambda i: (i, 0)),
            pl.BlockSpec((8, LANES), lambda i: (0, 0)),
        ),
        scratch_shapes=[pltpu.VMEM((1, LANES), F32)],
        compiler_params=_params("arbitrary"),
        name="router",
    )(x, g.reshape(1, d), modr, modr, rw_pad, rb_pad)


def _row_copy(src_hbm, dst, sem, src_row, dst_row):
    return pltpu.make_async_copy(src_hbm.at[pl.ds(src_row, 1), :], dst.at[pl.ds(dst_row, 1), :], sem)


def _dispatch_kernel(src_ref, nv_ref, h_hbm, o_ref, buf, sem):
    tm = o_ref.shape[0]
    i = pl.program_id(0)

    @pl.when(i < nv_ref[0])
    def _():
        base = i * tm

        def issue(r, c):
            _row_copy(h_hbm, buf, sem, src_ref[base + r], r).start()
            return c

        lax.fori_loop(0, tm, issue, 0)

        def drain(r, c):
            _row_copy(h_hbm, buf, sem, 0, r).wait()
            return c

        lax.fori_loop(0, tm, drain, 0)
        o_ref[...] = buf[...].astype(o_ref.dtype)

    @pl.when(i >= nv_ref[0])
    def _():
        o_ref[...] = jnp.zeros_like(o_ref)


def _dispatch(h, src_tok, n_valid, n_tiles, tm):
    d = h.shape[1]
    return pl.pallas_call(
        _dispatch_kernel,
        out_shape=SDS((n_tiles * tm, d), BF16),
        grid_spec=pltpu.PrefetchScalarGridSpec(
            num_scalar_prefetch=2,
            grid=(n_tiles,),
            in_specs=[pl.BlockSpec(memory_space=pl.ANY)],
            out_specs=pl.BlockSpec((tm, d), lambda i, s, nv: (i, 0)),
            scratch_shapes=[pltpu.VMEM((tm, d), F32), pltpu.SemaphoreType.DMA],
        ),
        compiler_params=_params("arbitrary"),
        name="moe_dispatch",
    )(src_tok, n_valid, h)


def _gmm_up_kernel(te_ref, tb_ref, nv_ref, x_ref, wg_ref, wu_ref, bg_ref, bu_ref, o_ref):
    i = pl.program_id(1)

    @pl.when(i < nv_ref[0])
    def _():
        x = x_ref[...]
        g = jnp.minimum(_dot(x, wg_ref[...]) + bg_ref[...], SWIGLU_LIMIT)
        u = jnp.clip(_dot(x, wu_ref[...]) + bu_ref[...], -SWIGLU_LIMIT, SWIGLU_LIMIT)
        o_ref[...] = ((u + 1.0) * g * jax.nn.sigmoid(SWIGLU_ALPHA * g)).astype(o_ref.dtype)

    @pl.when(i >= nv_ref[0])
    def _():
        o_ref[...] = jnp.zeros_like(o_ref)


def _gmm_up(xs, wg, wu, bg, bu, tile_e, tile_b, n_valid, tm, tn):
    ns, d = xs.shape
    n_exp, _, de = wg.shape
    n_tiles = ns // tm
    return pl.pallas_call(
        _gmm_up_kernel,
        out_shape=SDS((ns, de), BF16),
        grid_spec=pltpu.PrefetchScalarGridSpec(
            num_scalar_prefetch=3,
            grid=(de // tn, n_tiles),
            in_specs=[
                pl.BlockSpec((tm, d), lambda j, i, te, tb, nv: (tb[i], 0)),
                pl.BlockSpec((None, d, tn), lambda j, i, te, tb, nv: (te[i], 0, j)),
                pl.BlockSpec((None, d, tn), lambda j, i, te, tb, nv: (te[i], 0, j)),
                pl.BlockSpec((None, 1, tn), lambda j, i, te, tb, nv: (te[i], 0, j)),
                pl.BlockSpec((None, 1, tn), lambda j, i, te, tb, nv: (te[i], 0, j)),
            ],
            out_specs=pl.BlockSpec((tm, tn), lambda j, i, te, tb, nv: (i, j)),
        ),
        compiler_params=_params("arbitrary", "arbitrary"),
        name="moe_up",
    )(tile_e, tile_b, n_valid, xs, wg, wu, bg.reshape(n_exp, 1, de), bu.reshape(n_exp, 1, de))


def _gmm_down_kernel(te_ref, tb_ref, nv_ref, a_ref, w_ref, b_ref, o_ref):
    i = pl.program_id(1)

    @pl.when(i < nv_ref[0])
    def _():
        o_ref[...] = _dot(a_ref[...], w_ref[...]) + b_ref[...]

    @pl.when(i >= nv_ref[0])
    def _():
        o_ref[...] = jnp.zeros_like(o_ref)


def _gmm_down(act, wd, bd, tile_e, tile_b, n_valid, tm, tn):
    ns, de = act.shape
    n_exp, _, d = wd.shape
    n_tiles = ns // tm
    return pl.pallas_call(
        _gmm_down_kernel,
        out_shape=SDS((ns, d), F32),
        grid_spec=pltpu.PrefetchScalarGridSpec(
            num_scalar_prefetch=3,
            grid=(d // tn, n_tiles),
            in_specs=[
                pl.BlockSpec((tm, de), lambda j, i, te, tb, nv: (tb[i], 0)),
                pl.BlockSpec((None, de, tn), lambda j, i, te, tb, nv: (te[i], 0, j)),
                pl.BlockSpec((None, 1, tn), lambda j, i, te, tb, nv: (te[i], 0, j)),
            ],
            out_specs=pl.BlockSpec((tm, tn), lambda j, i, te, tb, nv: (i, j)),
        ),
        compiler_params=_params("arbitrary", "arbitrary"),
        name="moe_down",
    )(tile_e, tile_b, n_valid, act, wd, bd.reshape(n_exp, 1, d))


def _combine_kernel(pos_ref, y_hbm, x_ref, w_ref, g_ref, fg_ref, o_ref, buf, sem, *, final_norm):
    tm = x_ref.shape[0]
    base = pl.program_id(0) * tm * TOP_K

    def issue(r, c):
        for k in range(TOP_K):
            _row_copy(y_hbm, buf.at[k], sem, pos_ref[base + r * TOP_K + k], r).start()
        return c

    lax.fori_loop(0, tm, issue, 0)

    def drain(r, c):
        for k in range(TOP_K):
            _row_copy(y_hbm, buf.at[k], sem, 0, r).wait()
        return c

    lax.fori_loop(0, tm, drain, 0)
    w = w_ref[...]
    acc = w[:, 0:1] * buf[0]
    for k in range(1, TOP_K):
        acc = acc + w[:, k:k + 1] * buf[k]
    x = x_ref[...] + g_ref[...] * acc
    if final_norm:
        ms = jnp.mean(x * x, axis=-1, keepdims=True)
        x = x * lax.rsqrt(ms + EPS) * fg_ref[...]
    o_ref[...] = x


def _combine(y_sorted, pos_flat, x, w_pad, modr, g_idx, cond_of_tile, final_g, final_norm, tm):
    n, d = x.shape
    return pl.pallas_call(
        functools.partial(_combine_kernel, final_norm=final_norm),
        out_shape=SDS((n, d), F32),
        grid_spec=pltpu.PrefetchScalarGridSpec(
            num_scalar_prefetch=1,
            grid=(n // tm,),
            in_specs=[
                pl.BlockSpec(memory_space=pl.ANY),
                pl.BlockSpec((tm, d), lambda i, p: (i, 0)),
                pl.BlockSpec((tm, LANES), lambda i, p: (i, 0)),
                pl.BlockSpec((None, None, 1, d), lambda i, p: (cond_of_tile(i, tm), g_idx, 0, 0)),
                pl.BlockSpec((1, d), lambda i, p: (0, 0)),
            ],
            out_specs=pl.BlockSpec((tm, d), lambda i, p: (i, 0)),
            scratch_shapes=[pltpu.VMEM((TOP_K, tm, d), F32), pltpu.SemaphoreType.DMA],
        ),
        compiler_params=_params("arbitrary"),
        name="moe_combine",
    )(pos_flat, y_sorted, x, w_pad, modr, final_g.reshape(1, d))


def _pick_tile(n, pref):
    t = min(pref, n)
    while n % t:
        t //= 2
    return t


def kernel(x_prompt, x_sample, c, state_lru, state_ssd, c_ctx, ada_w, ada_b, norm1_g, norm2_g, w_in, lru_conv_w, lru_conv_b, lru_wa, lru_ba, lru_wi, lru_bi, lru_lambda, ssd_conv_w, ssd_conv_b, ssd_dt_bias, ssd_a_log, ssd_d, ssd_norm_g, w_out, router_w, router_b, exp_w_gate, exp_b_gate, exp_w_up, exp_b_up, exp_w_down, exp_b_down, final_norm_g):
    b_ctx, t_ctx, d = x_prompt.shape
    b_lat, t_lat, _ = x_sample.shape
    depth = ada_w.shape[0]
    n_ctx, n_lat = b_ctx * t_ctx, b_lat * t_lat
    n = n_ctx + n_lat
    lru_w = lru_conv_w.shape[-1]
    _, _, heads, hp, nstate = state_ssd.shape[1:]
    inner = heads * hp
    conv_ch = ssd_conv_w.shape[-1]
    groups = (conv_ch - inner) // (2 * nstate)
    gw = inner // groups
    hg = heads // groups
    n_experts = router_w.shape[-1]
    n_blocks, lru_blk = lru_wa.shape[2], lru_wa.shape[3]
    assert lru_blk == LANES and hg <= 8 and LANES % hp == 0 and n_experts <= LANES
    assert t_ctx % SSD_CHUNK == 0 and t_lat % SSD_CHUNK == 0 and n_ctx % t_lat == 0
    assert b_lat + 1 <= 8

    off_z = 2 * lru_w
    off_x = off_z + inner
    off_b = off_x + inner
    off_c = off_b + groups * nstate
    n_main = off_c + groups * nstate
    ssd_dims = (off_z, off_x, off_b, off_c, inner, groups, nstate)

    tm = _pick_tile(min(t_ctx, t_lat), 512)
    cond_of_tile = functools.partial(_cond_row, n_ctx=n_ctx, t_lat=t_lat)

    x = jnp.concatenate([x_prompt.reshape(n_ctx, d), x_sample.reshape(n_lat, d)], axis=0)
    cond8 = jnp.zeros((8, d), F32).at[0].set(c_ctx).at[1:1 + b_lat].set(c)
    mod = _ada_mod(cond8, ada_w, ada_b)
    modr_all = mod[:, :1 + b_lat].reshape(depth, 1 + b_lat, 6, 1, d)

    tn_in = _pick_tile(n_main, 1024)
    tn_out = _pick_tile(d, 1024)
    moe_tm = MOE_TILE
    n_tiles = (n * TOP_K + n_experts * (moe_tm - 1)) // moe_tm + 1
    tok_ids = jnp.repeat(jnp.arange(n, dtype=I32), TOP_K)
    rw_pad = jnp.zeros((depth, d, LANES), F32).at[:, :, :n_experts].set(router_w)
    rb_pad = jnp.zeros((depth, 1, LANES), F32).at[:, 0, :n_experts].set(router_b)

    lru_states, ssd_states = [], []
    zeros_lru = jnp.zeros((b_ctx, 2, lru_w), F32)
    zeros_ssd = jnp.zeros((b_ctx, 2, groups, gw, nstate), F32)
    y_final = None
    for l in range(depth):
        modr = modr_all[l]
        h = _normmod(x, norm1_g[l], modr, 0, 1, cond_of_tile, tm)
        w_in_bf = w_in[l].astype(BF16)
        proj = _matmul(h, w_in_bf[:, :n_main], tm, tn_in)
        w_dt = jnp.zeros((d, LANES), BF16).at[:, :heads].set(w_in_bf[:, n_main:])
        dt_raw = _matmul(h, w_dt, tm, LANES)[:, :heads]
        dtg = jnp.zeros((groups, n, LANES), F32).at[:, :, :hg].set(
            dt_raw.reshape(n, groups, hg).transpose(1, 0, 2))
        dtgt = dt_raw.reshape(n // SSD_CHUNK, SSD_CHUNK, groups, hg).transpose(2, 0, 3, 1)

        wg = jnp.concatenate([lru_wa[l, 0], lru_wi[l, 0], lru_wa[l, 1], lru_wi[l, 1]], axis=-1).astype(BF16)
        gb = jnp.stack([lru_ba[l, 0], lru_bi[l, 0], lru_ba[l, 1], lru_bi[l, 1]], axis=0)
        lru_args = (lru_conv_w[l], lru_conv_b[l].reshape(1, lru_w), wg, gb, lru_lambda[l])
        y_lru_c, h_c = _lru_group(proj, 0, b_ctx, t_ctx, zeros_lru, *lru_args)
        y_lru_s, _ = _lru_group(proj, n_ctx, b_lat, t_lat, state_lru[:, l], *lru_args)

        def per_group(p, fill):
            pg = p.reshape(2, groups, hg).transpose(1, 0, 2)
            rowf = jnp.full((groups, 2, 1, LANES), fill, F32).at[:, :, 0, :hg].set(pg)
            return rowf, pg.reshape(groups, 2, hg, 1)

        dtb, dtbt = per_group(ssd_dt_bias[l], 0.0)
        alog, alogt = per_group(ssd_a_log[l], 0.0)
        dsum = jnp.repeat((ssd_d[l, 0] + ssd_d[l, 1]).reshape(groups, 1, hg), hp, axis=-1)
        ssd_args = (ssd_conv_w[l], ssd_conv_b[l].reshape(1, conv_ch), dtb, dtbt, alog, alogt, dsum,
                    ssd_norm_g[l].reshape(1, inner), ssd_dims)
        y_ssd_c, s_c = _ssd_group(proj, dtg, dtgt, 0, b_ctx, t_ctx, zeros_ssd, *ssd_args)
        s0_lat = state_ssd[:, l].reshape(b_lat, 2, groups, gw, nstate)
        y_ssd_s, _ = _ssd_group(proj, dtg, dtgt, n_ctx, b_lat, t_lat, s0_lat, *ssd_args)
        lru_states.append(h_c)
        ssd_states.append(s_c.reshape(b_ctx, 2, heads, hp, nstate))

        y_lru = jnp.concatenate([y_lru_c, y_lru_s], axis=0)
        y_ssd = jnp.concatenate([y_ssd_c, y_ssd_s], axis=0)
        x = _wout(y_lru, y_ssd, w_out[l].astype(BF16), x, modr, 2, cond_of_tile, tm, tn_out)

        rt = _pick_tile(tm, 256)
        h2, w_pad, idx_pad, rank_pad, cnt = _router(x, norm2_g[l], modr, 3, 4, cond_of_tile,
                                                     rw_pad[l], rb_pad[l], n_experts, rt)
        counts = cnt[0, :n_experts].astype(I32)
        padded = (counts + moe_tm - 1) // moe_tm * moe_tm
        ends = jnp.cumsum(padded)
        starts = ends - padded
        top_i = idx_pad[:, :TOP_K]
        pos = (starts[top_i] + rank_pad[:, :TOP_K]).reshape(-1)
        src_tok = jnp.zeros((n_tiles * moe_tm,), I32).at[pos].set(tok_ids)
        n_valid = (ends[-1] // moe_tm).reshape(1).astype(I32)
        tile_ids = jnp.arange(n_tiles, dtype=I32)
        last_valid = jnp.maximum(n_valid[0] - 1, 0)
        tile_b = jnp.minimum(tile_ids, last_valid)
        tile_e = jnp.minimum(jnp.searchsorted(ends, tile_b * moe_tm, side="right"), n_experts - 1).astype(I32)

        xs = _dispatch(h2, src_tok, n_valid, n_tiles, moe_tm)
        act = _gmm_up(xs, exp_w_gate[l].astype(BF16), exp_w_up[l].astype(BF16), exp_b_gate[l], exp_b_up[l],
                      tile_e, tile_b, n_valid, moe_tm, _pick_tile(exp_w_gate.shape[-1], 512))
        y_sorted = _gmm_down(act, exp_w_down[l].astype(BF16), exp_b_down[l], tile_e, tile_b, n_valid,
                             moe_tm, _pick_tile(d, 1024))
        last = l == depth - 1
        x = _combine(y_sorted, pos, x, w_pad, modr, 5, cond_of_tile, final_norm_g, last,
                     _pick_tile(tm, COMBINE_TILE))
        if last:
            y_final = x

    y_prompt = y_final[:n_ctx].reshape(b_ctx, t_ctx, d)
    y_sample = y_final[n_ctx:].reshape(b_lat, t_lat, d)
    return (y_prompt, y_sample, jnp.stack(lru_states, axis=1), jnp.stack(ssd_states, axis=1))
```

```python
import functools

import jax
import jax.numpy as jnp
from jax import lax
from jax.experimental import pallas as pl
from jax.experimental.pallas import tpu as pltpu

F32 = jnp.float32
BF16 = jnp.bfloat16
I32 = jnp.int32
HIGHEST = lax.Precision.HIGHEST

TOP_K = 4
LRU_C = 8.0
SWIGLU_LIMIT = 7.0
SWIGLU_ALPHA = 1.702
EPS = 1e-6
SSD_CHUNK = 128
LANES = 128
NEG_BIG = -1e30
VMEM_LIMIT_BYTES = 56 * 1024 * 1024
MOE_TILE = 256
COMBINE_TILE = 128
LRU_SCRATCH_BYTES = 2 * 1024 * 1024
CAST_ROWS = 512
MOE_UP_COLS = 512
MOE_DOWN_COLS = 2048

SDS = jax.ShapeDtypeStruct


def _params(*sem):
    return pltpu.CompilerParams(dimension_semantics=sem, vmem_limit_bytes=VMEM_LIMIT_BYTES)


def _dot(a, b):
    return jnp.dot(a, b, preferred_element_type=F32)


def _dot_exact(a, b):
    return jnp.dot(a, b, precision=HIGHEST, preferred_element_type=F32)


def _split3(x):
    hi = x.astype(BF16)
    r1 = x - hi.astype(F32)
    mid = r1.astype(BF16)
    lo = (r1 - mid.astype(F32)).astype(BF16)
    return hi, mid, lo


def _dot_sel_right(x, sel):
    hi, mid, lo = _split3(x)
    return _dot(hi, sel) + _dot(mid, sel) + _dot(lo, sel)


def _dot_sel_left(sel, x):
    hi, mid, lo = _split3(x)
    return _dot(sel, hi) + _dot(sel, mid) + _dot(sel, lo)


def _softplus(x):
    return jnp.maximum(x, 0.0) + jnp.log1p(jnp.exp(-jnp.abs(x)))


def _silu(x):
    return x * jax.nn.sigmoid(x)


def _cond_row(i, tm, n_ctx, t_lat):
    r = i * tm
    return jnp.where(r < n_ctx, 0, 1 + (r - n_ctx) // t_lat)


def _ada_kernel(c_ref, w_ref, b_ref, o_ref):
    c = c_ref[...]
    o_ref[...] = _dot_exact(_silu(c), w_ref[...]) + b_ref[...]


def _ada_mod(cond8, ada_w, ada_b):
    depth, d, d6 = ada_w.shape
    tn = min(512, d6)
    return pl.pallas_call(
        _ada_kernel,
        out_shape=SDS((depth, 8, d6), F32),
        grid=(depth, d6 // tn),
        in_specs=[
            pl.BlockSpec((8, d), lambda l, j: (0, 0)),
            pl.BlockSpec((None, d, tn), lambda l, j: (l, 0, j)),
            pl.BlockSpec((None, 1, tn), lambda l, j: (l, 0, j)),
        ],
        out_specs=pl.BlockSpec((None, 8, tn), lambda l, j: (l, 0, j)),
        compiler_params=_params("arbitrary", "arbitrary"),
        name="ada_mod",
    )(cond8, ada_w, ada_b.reshape(depth, 1, d6))


def _norm_mod(x, g, sc, sh):
    ms = jnp.mean(x * x, axis=-1, keepdims=True)
    return (x * lax.rsqrt(ms + EPS) * g) * (1.0 + sc) + sh


def _pick_part(i, n_first, xa_ref, xb_ref):
    return jnp.where(i < n_first, xa_ref[...], xb_ref[...])


def _part_maps(n_first, col):
    return (lambda *a: (jnp.minimum(a[-1], n_first - 1), col(*a)),
            lambda *a: (jnp.maximum(a[-1] - n_first, 0), col(*a)))


def _normmod_kernel(xa_ref, xb_ref, g_ref, sc_ref, sh_ref, o_ref, *, n_first):
    x = _pick_part(pl.program_id(0), n_first, xa_ref, xb_ref)
    o_ref[...] = _norm_mod(x, g_ref[...], sc_ref[...], sh_ref[...]).astype(o_ref.dtype)


def _normmod(xa, xb, g, modr, sh_idx, sc_idx, cond_of_tile, tm):
    d = xa.shape[1]
    n = xa.shape[0] if xa is xb else xa.shape[0] + xb.shape[0]
    n_first = xa.shape[0] // tm
    map_a, map_b = _part_maps(n_first, lambda i: 0)
    return pl.pallas_call(
        functools.partial(_normmod_kernel, n_first=n_first),
        out_shape=SDS((n, d), BF16),
        grid=(n // tm,),
        in_specs=[
            pl.BlockSpec((tm, d), map_a),
            pl.BlockSpec((tm, d), map_b),
            pl.BlockSpec((1, d), lambda i: (0, 0)),
            pl.BlockSpec((None, None, 1, d), lambda i: (cond_of_tile(i, tm), sc_idx, 0, 0)),
            pl.BlockSpec((None, None, 1, d), lambda i: (cond_of_tile(i, tm), sh_idx, 0, 0)),
        ],
        out_specs=pl.BlockSpec((tm, d), lambda i: (i, 0)),
        compiler_params=_params("arbitrary"),
        name="norm_mod",
    )(xa, xb, g.reshape(1, d), modr, modr)


def _mm_kernel(a_ref, w_ref, o_ref):
    o_ref[...] = _dot(a_ref[...], w_ref[...]).astype(o_ref.dtype)


def _matmul(a, w, tm, tn, out_dtype=F32):
    n, k = a.shape
    m = w.shape[1]
    return pl.pallas_call(
        _mm_kernel,
        out_shape=SDS((n, m), out_dtype),
        grid=(m // tn, n // tm),
        in_specs=[
            pl.BlockSpec((tm, k), lambda j, i: (i, 0)),
            pl.BlockSpec((k, tn), lambda j, i: (0, j)),
        ],
        out_specs=pl.BlockSpec((tm, tn), lambda j, i: (i, j)),
        compiler_params=_params("arbitrary", "arbitrary"),
        name="matmul",
    )(a, w)


def _wout_kernel(a1_ref, a2_ref, w1_ref, w2_ref, xa_ref, xb_ref, g_ref, o_ref, *, n_first):
    m = _dot(a1_ref[...], w1_ref[...]) + _dot(a2_ref[...], w2_ref[...])
    o_ref[...] = _pick_part(pl.program_id(1), n_first, xa_ref, xb_ref) + g_ref[...] * m


def _wout(y1, y2, w, xa, xb, modr, g_idx, cond_of_tile, tm, tn):
    n, k1 = y1.shape
    k2 = y2.shape[1]
    d = w.shape[1]
    assert k1 == k2
    n_first = xa.shape[0] // tm
    map_a, map_b = _part_maps(n_first, lambda j, i: j)
    return pl.pallas_call(
        functools.partial(_wout_kernel, n_first=n_first),
        out_shape=SDS((n, d), F32),
        grid=(d // tn, n // tm),
        in_specs=[
            pl.BlockSpec((tm, k1), lambda j, i: (i, 0)),
            pl.BlockSpec((tm, k2), lambda j, i: (i, 0)),
            pl.BlockSpec((k1, tn), lambda j, i: (0, j)),
            pl.BlockSpec((k2, tn), lambda j, i: (1, j)),
            pl.BlockSpec((tm, tn), map_a),
            pl.BlockSpec((tm, tn), map_b),
            pl.BlockSpec((None, None, 1, tn), lambda j, i: (cond_of_tile(i, tm), g_idx, 0, j)),
        ],
        out_specs=pl.BlockSpec((tm, tn), lambda j, i: (i, j)),
        compiler_params=_params("arbitrary", "arbitrary"),
        name="w_out",
    )(y1, y2, w, w, xa, xb, modr)


def _shift_rows(x, s):
    t_len = x.shape[0]
    rolled = pltpu.roll(x, s % t_len, axis=0)
    t = lax.broadcasted_iota(I32, x.shape, 0)
    keep = (t >= s) if s > 0 else (t < t_len + s)
    return jnp.where(keep, rolled, 0.0)


def _dwconv(x, w_ref, b_ref):
    return (b_ref[...] + w_ref[2:3, :] * x + w_ref[1:2, :] * _shift_rows(x, 1)
            + w_ref[0:1, :] * _shift_rows(x, 2) + w_ref[3:4, :] * _shift_rows(x, -1))


def _lru_kernel(x_ref, g_ref, cw_ref, cb_ref, wg_ref, gb_ref, lam_ref, h0_ref,
                y_ref, hfin_ref, af_s, bf_s, ab_s, bb_s):
    t_len, tc = x_ref.shape
    xc = _dwconv(x_ref[...], cw_ref, cb_ref)
    for n in range(tc // LANES):
        sl = slice(n * LANES, (n + 1) * LANES)
        xn = xc[:, sl]
        gates = _dot(xn.astype(BF16), wg_ref[n])
        for d, (a_s, b_s) in enumerate(((af_s, bf_s), (ab_s, bb_s))):
            r = jax.nn.sigmoid(gates[:, (2 * d) * LANES:(2 * d + 1) * LANES] + gb_ref[2 * d:2 * d + 1, sl])
            i = jax.nn.sigmoid(gates[:, (2 * d + 1) * LANES:(2 * d + 2) * LANES]
                               + gb_ref[2 * d + 1:2 * d + 2, sl])
            log_a = -LRU_C * r * _softplus(-lam_ref[d:d + 1, sl])
            a_s[:, sl] = jnp.exp(log_a)
            th = jnp.tanh(log_a)
            b_s[:, sl] = jnp.sqrt(-2.0 * th / (1.0 - th)) * (i * xn)

    def step(s, carry):
        hf, hb = carry
        hf = af_s[pl.ds(s, 1), :] * hf + bf_s[pl.ds(s, 1), :]
        bf_s[pl.ds(s, 1), :] = hf
        u = t_len - 1 - s
        hb = ab_s[pl.ds(u, 1), :] * hb + bb_s[pl.ds(u, 1), :]
        bb_s[pl.ds(u, 1), :] = hb
        return hf, hb

    hf, hb = lax.fori_loop(0, t_len, step, (h0_ref[0:1, :], h0_ref[1:2, :]), unroll=8)
    hfin_ref[0:1, :] = hf
    hfin_ref[1:2, :] = hb
    y_ref[...] = ((bf_s[...] + bb_s[...]) * jax.nn.gelu(g_ref[...], approximate=True)).astype(y_ref.dtype)


def _lru_group(proj, row_off, nb, t_len, h0, cw, cb, wg, gb, lam):
    w = cw.shape[1]
    tc = max(LANES, min(w, LRU_SCRATCH_BYTES // (4 * t_len)))
    rb = row_off // t_len
    nct = w // tc
    return pl.pallas_call(
        _lru_kernel,
        out_shape=(SDS((nb * t_len, w), BF16), SDS((nb, 2, w), F32)),
        grid=(nb, nct),
        in_specs=[
            pl.BlockSpec((t_len, tc), lambda b, j: (rb + b, j)),
            pl.BlockSpec((t_len, tc), lambda b, j: (rb + b, nct + j)),
            pl.BlockSpec((4, tc), lambda b, j: (0, j)),
            pl.BlockSpec((1, tc), lambda b, j: (0, j)),
            pl.BlockSpec((tc // LANES, LANES, 4 * LANES), lambda b, j: (j, 0, 0)),
            pl.BlockSpec((4, tc), lambda b, j: (0, j)),
            pl.BlockSpec((2, tc), lambda b, j: (0, j)),
            pl.BlockSpec((None, 2, tc), lambda b, j: (b, 0, j)),
        ],
        out_specs=(
            pl.BlockSpec((t_len, tc), lambda b, j: (b, j)),
            pl.BlockSpec((None, 2, tc), lambda b, j: (b, 0, j)),
        ),
        scratch_shapes=[pltpu.VMEM((t_len, tc), F32)] * 4,
        compiler_params=_params("arbitrary", "arbitrary"),
        name="rglru",
    )(proj, proj, cw, cb, wg, gb, lam, h0)


def _ssd_kernel(xs_ref, bm_ref, cm_ref, z_ref, dt_ref, dtt_ref,
                cwx_ref, cwb_ref, cwc_ref, cbx_ref, cbb_ref, cbc_ref,
                dtb_ref, dtbt_ref, alog_ref, alogt_ref, dsum_ref, ng_ref, s0_ref,
                y_ref, sfin_ref, xs_s, bm_s, cm_s, xd_s, dt_s, yacc_s, st_s):
    t_len, gw = xs_ref.shape
    q = SSD_CHUNK
    nc = t_len // q
    hg = dtt_ref.shape[1]
    hp = gw // hg
    per_blk = LANES // hp

    xs_s[...] = _silu(_dwconv(xs_ref[...], cwx_ref, cbx_ref))
    bm_s[...] = _silu(_dwconv(bm_ref[...], cwb_ref, cbb_ref)).astype(BF16)
    cm_s[...] = _silu(_dwconv(cm_ref[...], cwc_ref, cbc_ref)).astype(BF16)

    row = lax.broadcasted_iota(I32, (q, q), 0)
    col = lax.broadcasted_iota(I32, (q, q), 1)
    e_h = lax.broadcasted_iota(I32, (LANES, gw), 0)
    e_c = lax.broadcasted_iota(I32, (LANES, gw), 1)
    expand = (e_c // hp == e_h).astype(BF16)
    lane_in_blk = lax.broadcasted_iota(I32, (q, LANES), 1) // hp

    for d in range(2):
        causal = (row >= col) if d == 0 else (row <= col)
        tri = causal.astype(BF16)
        tri_t = ((col >= row) if d == 0 else (col <= row)).astype(F32)
        last = q - 1 if d == 0 else 0
        a_row = -jnp.exp(alog_ref[d])
        a_col = -jnp.exp(alogt_ref[d])
        dt_all = _softplus(dt_ref[...] + dtb_ref[d])
        dt_s[...] = dt_all
        xd_s[...] = xs_s[...] * _dot_sel_right(dt_all, expand)
        st_s[...] = s0_ref[d].T

        def chunk(ci, carry):
            c = ci if d == 0 else nc - 1 - ci
            c0 = pl.multiple_of(c * q, q)
            rows = pl.ds(c0, q)
            da = dt_s[rows, :] * a_row
            da_t = _softplus(dtt_ref[c] + dtbt_ref[d]) * a_col
            cum = _dot_sel_left(tri, da)
            cum_t = _dot_exact(da_t, tri_t)
            tot = cum[last:last + 1, :]
            expcum_x = _dot_sel_right(jnp.exp(cum), expand)
            decay_x = _dot_sel_right(jnp.exp(tot - cum), expand)
            tot_x = expcum_x[last:last + 1, :]
            bm = bm_s[rows, :]
            cm = cm_s[rows, :]
            xd = xd_s[rows, :]
            scores = lax.dot_general(cm, bm, (((1,), (1,)), ((), ())), preferred_element_type=F32)
            st = st_s[...]
            y_off = _dot(cm, st.astype(BF16)) * expcum_x
            for blk in range(gw // LANES):
                xd_blk = xd[:, blk * LANES:(blk + 1) * LANES]
                m_parts, x_parts = [], []
                for sub in range(per_blk):
                    h = blk * per_blk + sub
                    diff = cum[:, h:h + 1] - cum_t[h:h + 1, :]
                    lmat = jnp.exp(jnp.where(causal, diff, NEG_BIG))
                    m_parts.append((scores * lmat).astype(BF16))
                    x_parts.append(jnp.where(lane_in_blk == sub, xd_blk, 0.0).astype(BF16))
                y_diag = _dot(jnp.concatenate(m_parts, axis=1), jnp.concatenate(x_parts, axis=0))
                sl = slice(blk * LANES, (blk + 1) * LANES)
                contrib = y_diag + y_off[:, sl]
                if d == 0:
                    yacc_s[rows, sl] = contrib
                else:
                    yacc_s[rows, sl] = yacc_s[rows, sl] + contrib
            upd = lax.dot_general(bm, (xd * decay_x).astype(BF16), (((0,), (0,)), ((), ())),
                                  preferred_element_type=F32)
            st_s[...] = st * tot_x + upd
            return carry

        lax.fori_loop(0, nc, chunk, 0)
        sfin_ref[d] = st_s[...].T

    z = z_ref[...]
    y = (yacc_s[...] + dsum_ref[...] * xs_s[...]) * _silu(z)
    ms = jnp.mean(y * y, axis=-1, keepdims=True)
    y_ref[...] = (y * lax.rsqrt(ms + EPS) * ng_ref[...]).astype(y_ref.dtype)


def _ssd_group(proj, dt_pad, dtgt, row_off, nb, t_len, s0, cw, cb, dtb, dtbt, alog, alogt, dsum, ng, dims):
    off_z, off_x, off_b, off_c, inner, groups, nstate = dims
    gw = inner // groups
    hg = dtgt.shape[2]
    q = SSD_CHUNK
    nc = t_len // q
    rb = row_off // t_len
    cbk, cck = (off_b - off_x) // nstate, (off_c - off_x) // nstate
    return pl.pallas_call(
        _ssd_kernel,
        out_shape=(SDS((nb * t_len, inner), BF16), SDS((nb, 2, groups, gw, nstate), F32)),
        grid=(nb, groups),
        in_specs=[
            pl.BlockSpec((t_len, gw), lambda b, g: (rb + b, off_x // gw + g)),
            pl.BlockSpec((t_len, nstate), lambda b, g: (rb + b, off_b // nstate + g)),
            pl.BlockSpec((t_len, nstate), lambda b, g: (rb + b, off_c // nstate + g)),
            pl.BlockSpec((t_len, gw), lambda b, g: (rb + b, off_z // gw + g)),
            pl.BlockSpec((t_len, LANES), lambda b, g: (rb + b, g)),
            pl.BlockSpec((None, nc, hg, q), lambda b, g: (g, rb + b, 0, 0)),
            pl.BlockSpec((4, gw), lambda b, g: (0, g)),
            pl.BlockSpec((4, nstate), lambda b, g: (0, cbk + g)),
            pl.BlockSpec((4, nstate), lambda b, g: (0, cck + g)),
            pl.BlockSpec((1, gw), lambda b, g: (0, g)),
            pl.BlockSpec((1, nstate), lambda b, g: (0, cbk + g)),
            pl.BlockSpec((1, nstate), lambda b, g: (0, cck + g)),
            pl.BlockSpec((None, 2, 1, LANES), lambda b, g: (g, 0, 0, 0)),
            pl.BlockSpec((None, 2, hg, 1), lambda b, g: (g, 0, 0, 0)),
            pl.BlockSpec((None, 2, 1, LANES), lambda b, g: (g, 0, 0, 0)),
            pl.BlockSpec((None, 2, hg, 1), lambda b, g: (g, 0, 0, 0)),
            pl.BlockSpec((None, 1, gw), lambda b, g: (g, 0, 0)),
            pl.BlockSpec((1, gw), lambda b, g: (0, g)),
            pl.BlockSpec((None, 2, None, gw, nstate), lambda b, g: (b, 0, g, 0, 0)),
        ],
        out_specs=(
            pl.BlockSpec((t_len, gw), lambda b, g: (b, g)),
            pl.BlockSpec((None, 2, None, gw, nstate), lambda b, g: (b, 0, g, 0, 0)),
        ),
        scratch_shapes=[
            pltpu.VMEM((t_len, gw), F32),
            pltpu.VMEM((t_len, nstate), BF16),
            pltpu.VMEM((t_len, nstate), BF16),
            pltpu.VMEM((t_len, gw), F32),
            pltpu.VMEM((t_len, LANES), F32),
            pltpu.VMEM((t_len, gw), F32),
            pltpu.VMEM((nstate, gw), F32),
        ],
        compiler_params=_params("arbitrary", "arbitrary"),
        name="ssd",
    )(proj, proj, proj, proj, dt_pad, dtgt, cw, cw, cw, cb, cb, cb, dtb, dtbt, alog, alogt, dsum, ng, s0)


def _router_kernel(x_ref, g_ref, sc_ref, sh_ref, rw_ref, rb_ref,
                   h_ref, w_ref, i_ref, r_ref, cnt_ref, carry_s, *, n_experts):
    tm = x_ref.shape[0]

    @pl.when(pl.program_id(0) == 0)
    def _():
        carry_s[...] = jnp.zeros_like(carry_s)

    h = _norm_mod(x_ref[...], g_ref[...], sc_ref[...], sh_ref[...])
    h_ref[...] = h
    rw = rw_ref[...]
    rw_hi = rw.astype(BF16)
    rw_lo = (rw - rw_hi.astype(F32)).astype(BF16)
    h_hi = h.astype(BF16)
    h_lo = (h - h_hi.astype(F32)).astype(BF16)
    logits = _dot(h_hi, rw_hi) + _dot(h_lo, rw_hi) + _dot(h_hi, rw_lo) + rb_ref[...]
    lane = lax.broadcasted_iota(I32, (tm, LANES), 1)
    cur = jnp.where(lane < n_experts, logits, -jnp.inf)
    vals, idxs = [], []
    member = jnp.zeros((tm, LANES), F32)
    for _k in range(TOP_K):
        m = jnp.max(cur, axis=-1, keepdims=True)
        idx = jnp.min(jnp.where(cur == m, lane, LANES), axis=-1, keepdims=True)
        sel = lane == idx
        vals.append(m)
        idxs.append(idx)
        member = member + sel.astype(F32)
        cur = jnp.where(sel, -jnp.inf, cur)
    exps = [jnp.exp(v - vals[0]) for v in vals]
    denom = exps[0]
    for e in exps[1:]:
        denom = denom + e
    r_i = lax.broadcasted_iota(I32, (tm, tm), 0)
    c_i = lax.broadcasted_iota(I32, (tm, tm), 1)
    earlier = (c_i < r_i).astype(BF16)
    before = _dot(earlier, member.astype(BF16)) + carry_s[...]
    w_out = jnp.zeros((tm, LANES), F32)
    i_out = jnp.zeros((tm, LANES), I32)
    r_out = jnp.zeros((tm, LANES), I32)
    for k in range(TOP_K):
        rank = jnp.sum(jnp.where(lane == idxs[k], before, 0.0), axis=-1, keepdims=True)
        w_out = jnp.where(lane == k, exps[k] / denom, w_out)
        i_out = jnp.where(lane == k, idxs[k], i_out)
        r_out = jnp.where(lane == k, rank.astype(I32), r_out)
    w_ref[...] = w_out
    i_ref[...] = i_out
    r_ref[...] = r_out
    carry_s[...] = carry_s[...] + jnp.sum(member, axis=0, keepdims=True)
    cnt_ref[...] = jnp.broadcast_to(carry_s[...], cnt_ref.shape)


def _router(x, g, modr, sh_idx, sc_idx, cond_of_tile, rw_pad, rb_pad, n_experts, tm):
    n, d = x.shape
    return pl.pallas_call(
        functools.partial(_router_kernel, n_experts=n_experts),
        out_shape=(SDS((n, d), F32), SDS((n, LANES), F32), SDS((n, LANES), I32),
                   SDS((n, LANES), I32), SDS((8, LANES), F32)),
        grid=(n // tm,),
        in_specs=[
            pl.BlockSpec((tm, d), lambda i: (i, 0)),
            pl.BlockSpec((1, d), lambda i: (0, 0)),
            pl.BlockSpec((None, None, 1, d), lambda i: (cond_of_tile(i, tm), sc_idx, 0, 0)),
            pl.BlockSpec((None, None, 1, d), lambda i: (cond_of_tile(i, tm), sh_idx, 0, 0)),
            pl.BlockSpec((d, LANES), lambda i: (0, 0)),
            pl.BlockSpec((1, LANES), lambda i: (0, 0)),
        ],
        out_specs=(
            pl.BlockSpec((tm, d), lambda i: (i, 0)),
            pl.BlockSpec((tm, LANES), lambda i: (i, 0)),
            pl.BlockSpec((tm, LANES), lambda i: (i, 0)),
            pl.BlockSpec((tm, LANES), lambda i: (i, 0)),
            pl.BlockSpec((8, LANES), lambda i: (0, 0)),
        ),
        scratch_shapes=[pltpu.VMEM((1, LANES), F32)],
        compiler_params=_params("arbitrary"),
        name="router",
    )(x, g.reshape(1, d), modr, modr, rw_pad, rb_pad)


def _row_copy(src_hbm, dst, sem, src_row, dst_row):
    return pltpu.make_async_copy(src_hbm.at[pl.ds(src_row, 1), :], dst.at[pl.ds(dst_row, 1), :], sem)


def _dispatch_kernel(src_ref, nv_ref, h_hbm, o_ref, buf, sem):
    tm = o_ref.shape[0]
    i = pl.program_id(0)
    nv = nv_ref[0]

    def issue(tile, slot):
        base = tile * tm

        def body(r, c):
            _row_copy(h_hbm, buf.at[slot], sem.at[slot], src_ref[base + r], r).start()
            return c

        lax.fori_loop(0, tm, body, 0, unroll=8)

    def drain(slot):
        pltpu.make_async_copy(h_hbm.at[pl.ds(0, tm), :], buf.at[slot], sem.at[slot]).wait()

    @pl.when(jnp.logical_and(i == 0, nv > 0))
    def _():
        issue(0, 0)

    @pl.when(i + 1 < nv)
    def _():
        issue(i + 1, (i + 1) & 1)

    @pl.when(i < nv)
    def _():
        slot = i & 1
        drain(slot)
        o_ref[...] = buf[slot].astype(o_ref.dtype)

    @pl.when(i >= nv)
    def _():
        o_ref[...] = jnp.zeros_like(o_ref)


def _dispatch(h, src_tok, n_valid, n_tiles, tm):
    d = h.shape[1]
    return pl.pallas_call(
        _dispatch_kernel,
        out_shape=SDS((n_tiles * tm, d), BF16),
        grid_spec=pltpu.PrefetchScalarGridSpec(
            num_scalar_prefetch=2,
            grid=(n_tiles,),
            in_specs=[pl.BlockSpec(memory_space=pl.ANY)],
            out_specs=pl.BlockSpec((tm, d), lambda i, s, nv: (i, 0)),
            scratch_shapes=[pltpu.VMEM((2, tm, d), F32), pltpu.SemaphoreType.DMA((2,))],
        ),
        compiler_params=_params("arbitrary"),
        name="moe_dispatch",
    )(src_tok, n_valid, h)


def _weight_copy(w_hbm, buf, sem, layer, expert, col_blk, slot, tn):
    cols = pl.ds(pl.multiple_of(col_blk * tn, tn), tn)
    return pltpu.make_async_copy(w_hbm.at[layer, expert, :, cols], buf.at[slot], sem.at[slot])


def _group_weights(te_ref, first_ref, nxt_ref, gidx_ref, meta_ref, weights, layer, tn):
    j = pl.program_id(0)
    i = pl.program_id(1)
    nj = pl.num_programs(0)
    ng = meta_ref[1]

    @pl.when(first_ref[i] == 1)
    def _():
        slot = (gidx_ref[i] + j * ng) & 1

        @pl.when(jnp.logical_and(i == 0, j == 0))
        def _():
            for w_hbm, buf, _bf, sem in weights:
                _weight_copy(w_hbm, buf, sem, layer, te_ref[i], j, slot, tn).start()

        for w_hbm, buf, _bf, sem in weights:
            _weight_copy(w_hbm, buf, sem, layer, te_ref[i], j, slot, tn).wait()

        next_j = jnp.where(gidx_ref[i] == ng - 1, j + 1, j)

        @pl.when(next_j < nj)
        def _():
            for w_hbm, buf, _bf, sem in weights:
                _weight_copy(w_hbm, buf, sem, layer, nxt_ref[i], next_j, 1 - slot, tn).start()

        for _w, buf, bf, _sem in weights:
            k = bf.shape[0]
            rows_per = min(CAST_ROWS, k)

            def cast(c, carry, buf=buf, bf=bf, rows_per=rows_per):
                rows = pl.ds(pl.multiple_of(c * rows_per, rows_per), rows_per)
                bf[rows, :] = buf[slot, rows, :].astype(BF16)
                return carry

            lax.fori_loop(0, k // rows_per, cast, 0)


def _gmm_up_kernel(te_ref, first_ref, nxt_ref, gidx_ref, tb_ref, meta_ref,
                   x_ref, wg_hbm, wu_hbm, bg_ref, bu_ref, o_ref,
                   wg_buf, wu_buf, wg_bf, wu_bf, sem_g, sem_u, *, layer):
    i = pl.program_id(1)
    tn = o_ref.shape[1]

    @pl.when(i < meta_ref[0])
    def _():
        _group_weights(te_ref, first_ref, nxt_ref, gidx_ref, meta_ref,
                       [(wg_hbm, wg_buf, wg_bf, sem_g), (wu_hbm, wu_buf, wu_bf, sem_u)], layer, tn)
        x = x_ref[...]
        g = jnp.minimum(_dot(x, wg_bf[...]) + bg_ref[...], SWIGLU_LIMIT)
        u = jnp.clip(_dot(x, wu_bf[...]) + bu_ref[...], -SWIGLU_LIMIT, SWIGLU_LIMIT)
        o_ref[...] = ((u + 1.0) * g * jax.nn.sigmoid(SWIGLU_ALPHA * g)).astype(o_ref.dtype)

    @pl.when(i >= meta_ref[0])
    def _():
        o_ref[...] = jnp.zeros_like(o_ref)


def _gmm_up(xs, wg, wu, bg, bu, plan, layer, tm, tn):
    ns, d = xs.shape
    depth, n_exp, _, de = wg.shape
    n_tiles = ns // tm
    eoff = layer * n_exp
    return pl.pallas_call(
        functools.partial(_gmm_up_kernel, layer=layer),
        out_shape=SDS((ns, de), BF16),
        grid_spec=pltpu.PrefetchScalarGridSpec(
            num_scalar_prefetch=6,
            grid=(de // tn, n_tiles),
            in_specs=[
                pl.BlockSpec((tm, d), lambda j, i, te, fi, nx, gi, tb, mt: (tb[i], 0)),
                pl.BlockSpec(memory_space=pl.ANY),
                pl.BlockSpec(memory_space=pl.ANY),
                pl.BlockSpec((None, 1, tn), lambda j, i, te, fi, nx, gi, tb, mt: (eoff + te[i], 0, j)),
                pl.BlockSpec((None, 1, tn), lambda j, i, te, fi, nx, gi, tb, mt: (eoff + te[i], 0, j)),
            ],
            out_specs=pl.BlockSpec((tm, tn), lambda j, i, te, fi, nx, gi, tb, mt: (i, j)),
            scratch_shapes=[
                pltpu.VMEM((2, d, tn), F32), pltpu.VMEM((2, d, tn), F32),
                pltpu.VMEM((d, tn), BF16), pltpu.VMEM((d, tn), BF16),
                pltpu.SemaphoreType.DMA((2,)), pltpu.SemaphoreType.DMA((2,)),
            ],
        ),
        compiler_params=_params("arbitrary", "arbitrary"),
        name="moe_up",
    )(*plan, xs, wg, wu, bg.reshape(depth * n_exp, 1, de), bu.reshape(depth * n_exp, 1, de))


def _gmm_down_kernel(te_ref, first_ref, nxt_ref, gidx_ref, tb_ref, meta_ref,
                     a_ref, w_hbm, b_ref, o_ref, w_buf, w_bf, sem, *, layer):
    i = pl.program_id(1)
    tn = o_ref.shape[1]

    @pl.when(i < meta_ref[0])
    def _():
        _group_weights(te_ref, first_ref, nxt_ref, gidx_ref, meta_ref, [(w_hbm, w_buf, w_bf, sem)], layer, tn)
        o_ref[...] = _dot(a_ref[...], w_bf[...]) + b_ref[...]

    @pl.when(i >= meta_ref[0])
    def _():
        o_ref[...] = jnp.zeros_like(o_ref)


def _gmm_down(act, wd, bd, plan, layer, tm, tn):
    ns, de = act.shape
    depth, n_exp, _, d = wd.shape
    n_tiles = ns // tm
    eoff = layer * n_exp
    return pl.pallas_call(
        functools.partial(_gmm_down_kernel, layer=layer),
        out_shape=SDS((ns, d), F32),
        grid_spec=pltpu.PrefetchScalarGridSpec(
            num_scalar_prefetch=6,
            grid=(d // tn, n_tiles),
            in_specs=[
                pl.BlockSpec((tm, de), lambda j, i, te, fi, nx, gi, tb, mt: (tb[i], 0)),
                pl.BlockSpec(memory_space=pl.ANY),
                pl.BlockSpec((None, 1, tn), lambda j, i, te, fi, nx, gi, tb, mt: (eoff + te[i], 0, j)),
            ],
            out_specs=pl.BlockSpec((tm, tn), lambda j, i, te, fi, nx, gi, tb, mt: (i, j)),
            scratch_shapes=[
                pltpu.VMEM((2, de, tn), F32), pltpu.VMEM((de, tn), BF16), pltpu.SemaphoreType.DMA((2,)),
            ],
        ),
        compiler_params=_params("arbitrary", "arbitrary"),
        name="moe_down",
    )(*plan, act, wd, bd.reshape(depth * n_exp, 1, d))


def _combine_kernel(pos_ref, y_hbm, x_ref, w_ref, g_ref, fg_ref, *rest, final_norm, n_first):
    outs, (buf, sem) = rest[:-2], rest[-2:]
    tm = x_ref.shape[0]
    i = pl.program_id(0)
    n_steps = pl.num_programs(0)

    def issue(tile, slot):
        base = tile * tm * TOP_K

        def body(r, c):
            for k in range(TOP_K):
                _row_copy(y_hbm, buf.at[slot, k], sem.at[slot], pos_ref[base + r * TOP_K + k], r).start()
            return c

        lax.fori_loop(0, tm, body, 0, unroll=4)

    def drain(slot):
        for k in range(TOP_K):
            pltpu.make_async_copy(y_hbm.at[pl.ds(0, tm), :], buf.at[slot, k], sem.at[slot]).wait()

    @pl.when(i == 0)
    def _():
        issue(0, 0)

    @pl.when(i + 1 < n_steps)
    def _():
        issue(i + 1, (i + 1) & 1)

    slot = i & 1
    drain(slot)
    w = w_ref[...]
    acc = w[:, 0:1] * buf[slot, 0]
    for k in range(1, TOP_K):
        acc = acc + w[:, k:k + 1] * buf[slot, k]
    x = x_ref[...] + g_ref[...] * acc
    if final_norm:
        ms = jnp.mean(x * x, axis=-1, keepdims=True)
        x = x * lax.rsqrt(ms + EPS) * fg_ref[...]
        first_ref, rest_ref = outs

        @pl.when(i < n_first)
        def _():
            first_ref[...] = x

        @pl.when(i >= n_first)
        def _():
            rest_ref[...] = x
    else:
        outs[0][...] = x


def _combine(y_sorted, pos_flat, x, w_pad, modr, g_idx, cond_of_tile, final_g, final_norm, n_first_rows, tm):
    n, d = x.shape
    n_first = n_first_rows // tm
    if final_norm:
        out_shape = (SDS((n_first_rows, d), F32), SDS((n - n_first_rows, d), F32))
        out_specs = (pl.BlockSpec((tm, d), lambda i, p: (jnp.minimum(i, n_first - 1), 0)),
                     pl.BlockSpec((tm, d), lambda i, p: (jnp.maximum(i - n_first, 0), 0)))
    else:
        out_shape = SDS((n, d), F32)
        out_specs = pl.BlockSpec((tm, d), lambda i, p: (i, 0))
    return pl.pallas_call(
        functools.partial(_combine_kernel, final_norm=final_norm, n_first=n_first),
        out_shape=out_shape,
        grid_spec=pltpu.PrefetchScalarGridSpec(
            num_scalar_prefetch=1,
            grid=(n // tm,),
            in_specs=[
                pl.BlockSpec(memory_space=pl.ANY),
                pl.BlockSpec((tm, d), lambda i, p: (i, 0)),
                pl.BlockSpec((tm, LANES), lambda i, p: (i, 0)),
                pl.BlockSpec((None, None, 1, d), lambda i, p: (cond_of_tile(i, tm), g_idx, 0, 0)),
                pl.BlockSpec((1, d), lambda i, p: (0, 0)),
            ],
            out_specs=out_specs,
            scratch_shapes=[pltpu.VMEM((2, TOP_K, tm, d), F32), pltpu.SemaphoreType.DMA((2,))],
        ),
        compiler_params=_params("arbitrary"),
        name="moe_combine",
    )(pos_flat, y_sorted, x, w_pad, modr, final_g.reshape(1, d))


def _pick_tile(n, pref):
    t = min(pref, n)
    while n % t:
        t //= 2
    return t


def _routing_plan(counts, n_tiles, tm):
    n_experts = counts.shape[0]
    padded = (counts + tm - 1) // tm * tm
    ends = jnp.cumsum(padded)
    starts = ends - padded
    n_valid = ends[-1] // tm
    tile_ids = jnp.arange(n_tiles, dtype=I32)
    tile_b = jnp.minimum(tile_ids, jnp.maximum(n_valid - 1, 0))
    tile_e = jnp.minimum(jnp.sum(ends[None, :] <= (tile_b * tm)[:, None], axis=1), n_experts - 1).astype(I32)
    prev_e = jnp.concatenate([jnp.full((1,), -1, I32), tile_e[:-1]])
    first = ((tile_e != prev_e) & (tile_ids < n_valid)).astype(I32)
    nonempty = padded > 0
    gidx_of_e = jnp.cumsum(nonempty.astype(I32)) - 1
    n_groups = jnp.sum(nonempty.astype(I32))
    big = n_experts
    idx = jnp.where(nonempty, jnp.arange(n_experts, dtype=I32), big)
    at_or_after = lax.cummin(idx, reverse=True)
    after = jnp.concatenate([at_or_after[1:], jnp.full((1,), big, I32)])
    nxt_of_e = jnp.where(after == big, at_or_after[0], after)
    nxt_of_e = jnp.minimum(nxt_of_e, n_experts - 1).astype(I32)
    meta = jnp.stack([n_valid, n_groups]).astype(I32)
    plan = (tile_e, first, nxt_of_e[tile_e], gidx_of_e[tile_e].astype(I32), tile_b.astype(I32), meta)
    return starts, plan


def kernel(x_prompt, x_sample, c, state_lru, state_ssd, c_ctx, ada_w, ada_b, norm1_g, norm2_g, w_in, lru_conv_w, lru_conv_b, lru_wa, lru_ba, lru_wi, lru_bi, lru_lambda, ssd_conv_w, ssd_conv_b, ssd_dt_bias, ssd_a_log, ssd_d, ssd_norm_g, w_out, router_w, router_b, exp_w_gate, exp_b_gate, exp_w_up, exp_b_up, exp_w_down, exp_b_down, final_norm_g):
    b_ctx, t_ctx, d = x_prompt.shape
    b_lat, t_lat, _ = x_sample.shape
    depth = ada_w.shape[0]
    n_ctx, n_lat = b_ctx * t_ctx, b_lat * t_lat
    n = n_ctx + n_lat
    lru_w = lru_conv_w.shape[-1]
    _, _, heads, hp, nstate = state_ssd.shape[1:]
    inner = heads * hp
    conv_ch = ssd_conv_w.shape[-1]
    groups = (conv_ch - inner) // (2 * nstate)
    gw = inner // groups
    hg = heads // groups
    n_experts = router_w.shape[-1]
    lru_blk = lru_wa.shape[3]
    assert lru_blk == LANES and hg <= 8 and LANES % hp == 0 and n_experts <= LANES
    assert t_ctx % SSD_CHUNK == 0 and t_lat % SSD_CHUNK == 0 and n_ctx % t_lat == 0
    assert b_lat + 1 <= 8

    off_z = 2 * lru_w
    off_x = off_z + inner
    off_b = off_x + inner
    off_c = off_b + groups * nstate
    n_main = off_c + groups * nstate
    ssd_dims = (off_z, off_x, off_b, off_c, inner, groups, nstate)

    tm = _pick_tile(min(t_ctx, t_lat), 512)
    cond_of_tile = functools.partial(_cond_row, n_ctx=n_ctx, t_lat=t_lat)

    xa, xb = x_prompt.reshape(n_ctx, d), x_sample.reshape(n_lat, d)
    cond8 = jnp.zeros((8, d), F32).at[0].set(c_ctx).at[1:1 + b_lat].set(c)
    mod = _ada_mod(cond8, ada_w, ada_b)
    modr_all = mod[:, :1 + b_lat].reshape(depth, 1 + b_lat, 6, 1, d)

    tn_in = _pick_tile(n_main, 1024)
    tn_out = _pick_tile(d, 1024)
    moe_tm = MOE_TILE
    n_tiles = (n * TOP_K + n_experts * (moe_tm - 1)) // moe_tm + 1
    tok_ids = jnp.repeat(jnp.arange(n, dtype=I32), TOP_K)
    rw_pad = jnp.zeros((depth, d, LANES), F32).at[:, :, :n_experts].set(router_w)
    rb_pad = jnp.zeros((depth, 1, LANES), F32).at[:, 0, :n_experts].set(router_b)

    lru_states, ssd_states = [], []
    zeros_lru = jnp.zeros((b_ctx, 2, lru_w), F32)
    zeros_ssd = jnp.zeros((b_ctx, 2, groups, gw, nstate), F32)
    y_final = None
    for l in range(depth):
        modr = modr_all[l]
        h = _normmod(xa, xb, norm1_g[l], modr, 0, 1, cond_of_tile, tm)
        w_in_bf = w_in[l].astype(BF16)
        proj = _matmul(h, w_in_bf[:, :n_main], tm, tn_in)
        w_dt = jnp.zeros((d, groups, LANES), BF16).at[:, :, :hg].set(
            w_in_bf[:, n_main:].reshape(d, groups, hg)).reshape(d, groups * LANES)
        dt_pad = _matmul(h, w_dt, tm, _pick_tile(groups * LANES, 512))
        dtgt = dt_pad.reshape(n // SSD_CHUNK, SSD_CHUNK, groups, LANES)[..., :hg].transpose(2, 0, 3, 1)

        wg = jnp.concatenate([lru_wa[l, 0], lru_wi[l, 0], lru_wa[l, 1], lru_wi[l, 1]], axis=-1).astype(BF16)
        gb = jnp.stack([lru_ba[l, 0], lru_bi[l, 0], lru_ba[l, 1], lru_bi[l, 1]], axis=0)
        lru_args = (lru_conv_w[l], lru_conv_b[l].reshape(1, lru_w), wg, gb, lru_lambda[l])
        y_lru_c, h_c = _lru_group(proj, 0, b_ctx, t_ctx, zeros_lru, *lru_args)
        y_lru_s, _ = _lru_group(proj, n_ctx, b_lat, t_lat, state_lru[:, l], *lru_args)

        def per_group(p, fill):
            pg = p.reshape(2, groups, hg).transpose(1, 0, 2)
            rowf = jnp.full((groups, 2, 1, LANES), fill, F32).at[:, :, 0, :hg].set(pg)
            return rowf, pg.reshape(groups, 2, hg, 1)

        dtb, dtbt = per_group(ssd_dt_bias[l], 0.0)
        alog, alogt = per_group(ssd_a_log[l], 0.0)
        dsum = jnp.repeat((ssd_d[l, 0] + ssd_d[l, 1]).reshape(groups, 1, hg), hp, axis=-1)
        ssd_args = (ssd_conv_w[l], ssd_conv_b[l].reshape(1, conv_ch), dtb, dtbt, alog, alogt, dsum,
                    ssd_norm_g[l].reshape(1, inner), ssd_dims)
        y_ssd_c, s_c = _ssd_group(proj, dt_pad, dtgt, 0, b_ctx, t_ctx, zeros_ssd, *ssd_args)
        s0_lat = state_ssd[:, l].reshape(b_lat, 2, groups, gw, nstate)
        y_ssd_s, _ = _ssd_group(proj, dt_pad, dtgt, n_ctx, b_lat, t_lat, s0_lat, *ssd_args)
        lru_states.append(h_c)
        ssd_states.append(s_c.reshape(b_ctx, 2, heads, hp, nstate))

        y_lru = jnp.concatenate([y_lru_c, y_lru_s], axis=0)
        y_ssd = jnp.concatenate([y_ssd_c, y_ssd_s], axis=0)
        x = _wout(y_lru, y_ssd, w_out[l].astype(BF16), xa, xb, modr, 2, cond_of_tile, tm, tn_out)

        rt = _pick_tile(tm, 256)
        h2, w_pad, idx_pad, rank_pad, cnt = _router(x, norm2_g[l], modr, 3, 4, cond_of_tile,
                                                     rw_pad[l], rb_pad[l], n_experts, rt)
        starts, plan = _routing_plan(cnt[0, :n_experts].astype(I32), n_tiles, moe_tm)
        pos = (starts[idx_pad[:, :TOP_K]] + rank_pad[:, :TOP_K]).reshape(-1)
        src_tok = jnp.zeros((n_tiles * moe_tm,), I32).at[pos].set(tok_ids)
        n_valid = plan[-1][:1]

        xs = _dispatch(h2, src_tok, n_valid, n_tiles, moe_tm)
        act = _gmm_up(xs, exp_w_gate, exp_w_up, exp_b_gate, exp_b_up, plan, l, moe_tm,
                      _pick_tile(exp_w_gate.shape[-1], MOE_UP_COLS))
        y_sorted = _gmm_down(act, exp_w_down, exp_b_down, plan, l, moe_tm, _pick_tile(d, MOE_DOWN_COLS))
        last = l == depth - 1
        out = _combine(y_sorted, pos, x, w_pad, modr, 5, cond_of_tile, final_norm_g, last, n_ctx,
                       _pick_tile(tm, COMBINE_TILE))
        if last:
            y_final = out
        else:
            xa = xb = out

    y_prompt = y_final[0].reshape(b_ctx, t_ctx, d)
    y_sample = y_final[1].reshape(b_lat, t_lat, d)
    return (y_prompt, y_sample, jnp.stack(lru_states, axis=1), jnp.stack(ssd_states, axis=1))
```

```python
import functools

import jax
import jax.numpy as jnp
from jax import lax
from jax.experimental import pallas as pl
from jax.experimental.pallas import tpu as pltpu

F32 = jnp.float32
BF16 = jnp.bfloat16
I32 = jnp.int32
U32 = jnp.uint32
HIGHEST = lax.Precision.HIGHEST

TOP_K = 4
LRU_C = 8.0
SWIGLU_LIMIT = 7.0
SWIGLU_ALPHA = 1.702
EPS = 1e-6
SSD_CHUNK = 128
LANES = 128
NEG_BIG = -1e30
VMEM_LIMIT_BYTES = 56 * 1024 * 1024
MOE_TILE = 256
COMBINE_TILE = 256
LRU_SCRATCH_BYTES = 2 * 1024 * 1024
CAST_ROWS = 512
MOE_UP_COLS = 512
MOE_DOWN_COLS = 2048

SDS = jax.ShapeDtypeStruct


def _params(*sem):
    return pltpu.CompilerParams(dimension_semantics=sem, vmem_limit_bytes=VMEM_LIMIT_BYTES)


def _dot(a, b):
    return jnp.dot(a, b, preferred_element_type=F32)


def _dot_exact(a, b):
    return jnp.dot(a, b, precision=HIGHEST, preferred_element_type=F32)


def _split3(x):
    hi = x.astype(BF16)
    r1 = x - hi.astype(F32)
    mid = r1.astype(BF16)
    lo = (r1 - mid.astype(F32)).astype(BF16)
    return hi, mid, lo


def _dot_sel_right(x, sel):
    hi, mid, lo = _split3(x)
    return _dot(hi, sel) + _dot(mid, sel) + _dot(lo, sel)


def _dot_sel_left(sel, x):
    hi, mid, lo = _split3(x)
    return _dot(sel, hi) + _dot(sel, mid) + _dot(sel, lo)


def _pack_halves(x):
    c = x.shape[1] // 2
    lo = pltpu.bitcast(x[:, :c].astype(BF16).astype(F32), U32)
    hi = pltpu.bitcast(x[:, c:].astype(BF16).astype(F32), U32)
    return lax.shift_right_logical(lo, jnp.uint32(16)) | hi


def _unpack_halves(w):
    lo = pltpu.bitcast(lax.shift_left(w, jnp.uint32(16)), F32)
    hi = pltpu.bitcast(w & jnp.uint32(0xFFFF0000), F32)
    return lo, hi


def _softplus(x):
    return jnp.maximum(x, 0.0) + jnp.log1p(jnp.exp(-jnp.abs(x)))


def _silu(x):
    return x * jax.nn.sigmoid(x)


def _cond_row(i, tm, n_ctx, t_lat):
    r = i * tm
    return jnp.where(r < n_ctx, 0, 1 + (r - n_ctx) // t_lat)


def _ada_kernel(c_ref, w_ref, b_ref, o_ref):
    c = c_ref[...]
    o_ref[...] = _dot_exact(_silu(c), w_ref[...]) + b_ref[...]


def _ada_mod(cond8, ada_w, ada_b):
    depth, d, d6 = ada_w.shape
    tn = min(512, d6)
    return pl.pallas_call(
        _ada_kernel,
        out_shape=SDS((depth, 8, d6), F32),
        grid=(depth, d6 // tn),
        in_specs=[
            pl.BlockSpec((8, d), lambda l, j: (0, 0)),
            pl.BlockSpec((None, d, tn), lambda l, j: (l, 0, j)),
            pl.BlockSpec((None, 1, tn), lambda l, j: (l, 0, j)),
        ],
        out_specs=pl.BlockSpec((None, 8, tn), lambda l, j: (l, 0, j)),
        compiler_params=_params("arbitrary", "arbitrary"),
        name="ada_mod",
    )(cond8, ada_w, ada_b.reshape(depth, 1, d6))


def _norm_mod(x, g, sc, sh):
    ms = jnp.mean(x * x, axis=-1, keepdims=True)
    return (x * lax.rsqrt(ms + EPS) * g) * (1.0 + sc) + sh


def _pick_part(i, n_first, xa_ref, xb_ref):
    return jnp.where(i < n_first, xa_ref[...], xb_ref[...])


def _part_maps(n_first, col):
    return (lambda *a: (jnp.minimum(a[-1], n_first - 1), col(*a)),
            lambda *a: (jnp.maximum(a[-1] - n_first, 0), col(*a)))


def _normmod_kernel(xa_ref, xb_ref, g_ref, sc_ref, sh_ref, o_ref, *, n_first):
    x = _pick_part(pl.program_id(0), n_first, xa_ref, xb_ref)
    o_ref[...] = _norm_mod(x, g_ref[...], sc_ref[...], sh_ref[...]).astype(o_ref.dtype)


def _normmod(xa, xb, g, modr, sh_idx, sc_idx, cond_of_tile, tm):
    d = xa.shape[1]
    n = xa.shape[0] if xa is xb else xa.shape[0] + xb.shape[0]
    n_first = xa.shape[0] // tm
    map_a, map_b = _part_maps(n_first, lambda i: 0)
    return pl.pallas_call(
        functools.partial(_normmod_kernel, n_first=n_first),
        out_shape=SDS((n, d), BF16),
        grid=(n // tm,),
        in_specs=[
            pl.BlockSpec((tm, d), map_a),
            pl.BlockSpec((tm, d), map_b),
            pl.BlockSpec((1, d), lambda i: (0, 0)),
            pl.BlockSpec((None, None, 1, d), lambda i: (cond_of_tile(i, tm), sc_idx, 0, 0)),
            pl.BlockSpec((None, None, 1, d), lambda i: (cond_of_tile(i, tm), sh_idx, 0, 0)),
        ],
        out_specs=pl.BlockSpec((tm, d), lambda i: (i, 0)),
        compiler_params=_params("arbitrary"),
        name="norm_mod",
    )(xa, xb, g.reshape(1, d), modr, modr)


def _mm_kernel(a_ref, w_ref, o_ref):
    o_ref[...] = _dot(a_ref[...], w_ref[...]).astype(o_ref.dtype)


def _matmul(a, w, tm, tn, out_dtype=F32):
    n, k = a.shape
    m = w.shape[1]
    return pl.pallas_call(
        _mm_kernel,
        out_shape=SDS((n, m), out_dtype),
        grid=(m // tn, n // tm),
        in_specs=[
            pl.BlockSpec((tm, k), lambda j, i: (i, 0)),
            pl.BlockSpec((k, tn), lambda j, i: (0, j)),
        ],
        out_specs=pl.BlockSpec((tm, tn), lambda j, i: (i, j)),
        compiler_params=_params("arbitrary", "arbitrary"),
        name="matmul",
    )(a, w)


def _wout_kernel(a1_ref, a2_ref, w1_ref, w2_ref, xa_ref, xb_ref, g_ref, o_ref, *, n_first):
    m = _dot(a1_ref[...], w1_ref[...]) + _dot(a2_ref[...], w2_ref[...])
    o_ref[...] = _pick_part(pl.program_id(1), n_first, xa_ref, xb_ref) + g_ref[...] * m


def _wout(y1, y2, w, xa, xb, modr, g_idx, cond_of_tile, tm, tn):
    n, k1 = y1.shape
    k2 = y2.shape[1]
    d = w.shape[1]
    assert k1 == k2
    n_first = xa.shape[0] // tm
    map_a, map_b = _part_maps(n_first, lambda j, i: j)
    return pl.pallas_call(
        functools.partial(_wout_kernel, n_first=n_first),
        out_shape=SDS((n, d), F32),
        grid=(d // tn, n // tm),
        in_specs=[
            pl.BlockSpec((tm, k1), lambda j, i: (i, 0)),
            pl.BlockSpec((tm, k2), lambda j, i: (i, 0)),
            pl.BlockSpec((k1, tn), lambda j, i: (0, j)),
            pl.BlockSpec((k2, tn), lambda j, i: (1, j)),
            pl.BlockSpec((tm, tn), map_a),
            pl.BlockSpec((tm, tn), map_b),
            pl.BlockSpec((None, None, 1, tn), lambda j, i: (cond_of_tile(i, tm), g_idx, 0, j)),
        ],
        out_specs=pl.BlockSpec((tm, tn), lambda j, i: (i, j)),
        compiler_params=_params("arbitrary", "arbitrary"),
        name="w_out",
    )(y1, y2, w, w, xa, xb, modr)


def _shift_rows(x, s):
    t_len = x.shape[0]
    rolled = pltpu.roll(x, s % t_len, axis=0)
    t = lax.broadcasted_iota(I32, x.shape, 0)
    keep = (t >= s) if s > 0 else (t < t_len + s)
    return jnp.where(keep, rolled, 0.0)


def _dwconv(x, w_ref, b_ref):
    return (b_ref[...] + w_ref[2:3, :] * x + w_ref[1:2, :] * _shift_rows(x, 1)
            + w_ref[0:1, :] * _shift_rows(x, 2) + w_ref[3:4, :] * _shift_rows(x, -1))


def _lru_kernel(x_ref, g_ref, cw_ref, cb_ref, wg_ref, gb_ref, lam_ref, h0_ref,
                y_ref, hfin_ref, af_s, bf_s, ab_s, bb_s):
    t_len, tc = x_ref.shape
    xc = _dwconv(x_ref[...], cw_ref, cb_ref)
    for n in range(tc // LANES):
        sl = slice(n * LANES, (n + 1) * LANES)
        xn = xc[:, sl]
        gates = _dot(xn.astype(BF16), wg_ref[n])
        for d, (a_s, b_s) in enumerate(((af_s, bf_s), (ab_s, bb_s))):
            r = jax.nn.sigmoid(gates[:, (2 * d) * LANES:(2 * d + 1) * LANES] + gb_ref[2 * d:2 * d + 1, sl])
            i = jax.nn.sigmoid(gates[:, (2 * d + 1) * LANES:(2 * d + 2) * LANES]
                               + gb_ref[2 * d + 1:2 * d + 2, sl])
            log_a = -LRU_C * r * _softplus(-lam_ref[d:d + 1, sl])
            a_s[:, sl] = jnp.exp(log_a)
            th = jnp.tanh(log_a)
            b_s[:, sl] = jnp.sqrt(-2.0 * th / (1.0 - th)) * (i * xn)

    def step(s, carry):
        hf, hb = carry
        hf = af_s[pl.ds(s, 1), :] * hf + bf_s[pl.ds(s, 1), :]
        bf_s[pl.ds(s, 1), :] = hf
        u = t_len - 1 - s
        hb = ab_s[pl.ds(u, 1), :] * hb + bb_s[pl.ds(u, 1), :]
        bb_s[pl.ds(u, 1), :] = hb
        return hf, hb

    hf, hb = lax.fori_loop(0, t_len, step, (h0_ref[0:1, :], h0_ref[1:2, :]), unroll=8)
    hfin_ref[0:1, :] = hf
    hfin_ref[1:2, :] = hb
    y_ref[...] = ((bf_s[...] + bb_s[...]) * jax.nn.gelu(g_ref[...], approximate=True)).astype(y_ref.dtype)


def _lru_group(proj, row_off, nb, t_len, h0, cw, cb, wg, gb, lam):
    w = cw.shape[1]
    tc = max(LANES, min(w, LRU_SCRATCH_BYTES // (4 * t_len)))
    rb = row_off // t_len
    nct = w // tc
    return pl.pallas_call(
        _lru_kernel,
        out_shape=(SDS((nb * t_len, w), BF16), SDS((nb, 2, w), F32)),
        grid=(nb, nct),
        in_specs=[
            pl.BlockSpec((t_len, tc), lambda b, j: (rb + b, j)),
            pl.BlockSpec((t_len, tc), lambda b, j: (rb + b, nct + j)),
            pl.BlockSpec((4, tc), lambda b, j: (0, j)),
            pl.BlockSpec((1, tc), lambda b, j: (0, j)),
            pl.BlockSpec((tc // LANES, LANES, 4 * LANES), lambda b, j: (j, 0, 0)),
            pl.BlockSpec((4, tc), lambda b, j: (0, j)),
            pl.BlockSpec((2, tc), lambda b, j: (0, j)),
            pl.BlockSpec((None, 2, tc), lambda b, j: (b, 0, j)),
        ],
        out_specs=(
            pl.BlockSpec((t_len, tc), lambda b, j: (b, j)),
            pl.BlockSpec((None, 2, tc), lambda b, j: (b, 0, j)),
        ),
        scratch_shapes=[pltpu.VMEM((t_len, tc), F32)] * 4,
        compiler_params=_params("arbitrary", "arbitrary"),
        name="rglru",
    )(proj, proj, cw, cb, wg, gb, lam, h0)


def _ssd_kernel(xs_ref, bm_ref, cm_ref, z_ref, dt_ref, dtt_ref,
                cwx_ref, cwb_ref, cwc_ref, cbx_ref, cbb_ref, cbc_ref,
                dtb_ref, dtbt_ref, alog_ref, alogt_ref, dsum_ref, ng_ref, s0_ref,
                y_ref, sfin_ref, xs_s, bm_s, cm_s, xd_s, dt_s, yacc_s, st_s):
    t_len, gw = xs_ref.shape
    q = SSD_CHUNK
    nc = t_len // q
    hg = dtt_ref.shape[1]
    hp = gw // hg
    per_blk = LANES // hp

    xs_s[...] = _silu(_dwconv(xs_ref[...], cwx_ref, cbx_ref))
    bm_s[...] = _silu(_dwconv(bm_ref[...], cwb_ref, cbb_ref)).astype(BF16)
    cm_s[...] = _silu(_dwconv(cm_ref[...], cwc_ref, cbc_ref)).astype(BF16)

    row = lax.broadcasted_iota(I32, (q, q), 0)
    col = lax.broadcasted_iota(I32, (q, q), 1)
    e_h = lax.broadcasted_iota(I32, (LANES, gw), 0)
    e_c = lax.broadcasted_iota(I32, (LANES, gw), 1)
    expand = (e_c // hp == e_h).astype(BF16)
    lane_in_blk = lax.broadcasted_iota(I32, (q, LANES), 1) // hp

    for d in range(2):
        causal = (row >= col) if d == 0 else (row <= col)
        tri = causal.astype(BF16)
        tri_t = ((col >= row) if d == 0 else (col <= row)).astype(F32)
        last = q - 1 if d == 0 else 0
        a_row = -jnp.exp(alog_ref[d])
        a_col = -jnp.exp(alogt_ref[d])
        dt_all = _softplus(dt_ref[...] + dtb_ref[d])
        dt_s[...] = dt_all
        xd_s[...] = xs_s[...] * _dot_sel_right(dt_all, expand)
        st_s[...] = s0_ref[d].T

        def chunk(ci, carry):
            c = ci if d == 0 else nc - 1 - ci
            c0 = pl.multiple_of(c * q, q)
            rows = pl.ds(c0, q)
            da = dt_s[rows, :] * a_row
            da_t = _softplus(dtt_ref[c] + dtbt_ref[d]) * a_col
            cum = _dot_sel_left(tri, da)
            cum_t = _dot_exact(da_t, tri_t)
            tot = cum[last:last + 1, :]
            expcum_x = _dot_sel_right(jnp.exp(cum), expand)
            decay_x = _dot_sel_right(jnp.exp(tot - cum), expand)
            tot_x = expcum_x[last:last + 1, :]
            bm = bm_s[rows, :]
            cm = cm_s[rows, :]
            xd = xd_s[rows, :]
            scores = lax.dot_general(cm, bm, (((1,), (1,)), ((), ())), preferred_element_type=F32)
            st = st_s[...]
            y_off = _dot(cm, st.astype(BF16)) * expcum_x
            for blk in range(gw // LANES):
                xd_blk = xd[:, blk * LANES:(blk + 1) * LANES]
                m_parts, x_parts = [], []
                for sub in range(per_blk):
                    h = blk * per_blk + sub
                    diff = cum[:, h:h + 1] - cum_t[h:h + 1, :]
                    lmat = jnp.exp(jnp.where(causal, diff, NEG_BIG))
                    m_parts.append((scores * lmat).astype(BF16))
                    x_parts.append(jnp.where(lane_in_blk == sub, xd_blk, 0.0).astype(BF16))
                y_diag = _dot(jnp.concatenate(m_parts, axis=1), jnp.concatenate(x_parts, axis=0))
                sl = slice(blk * LANES, (blk + 1) * LANES)
                contrib = y_diag + y_off[:, sl]
                if d == 0:
                    yacc_s[rows, sl] = contrib
                else:
                    yacc_s[rows, sl] = yacc_s[rows, sl] + contrib
            upd = lax.dot_general(bm, (xd * decay_x).astype(BF16), (((0,), (0,)), ((), ())),
                                  preferred_element_type=F32)
            st_s[...] = st * tot_x + upd
            return carry

        lax.fori_loop(0, nc, chunk, 0)
        sfin_ref[d] = st_s[...].T

    z = z_ref[...]
    y = (yacc_s[...] + dsum_ref[...] * xs_s[...]) * _silu(z)
    ms = jnp.mean(y * y, axis=-1, keepdims=True)
    y_ref[...] = (y * lax.rsqrt(ms + EPS) * ng_ref[...]).astype(y_ref.dtype)


def _ssd_group(proj, dt_pad, dtgt, row_off, nb, t_len, s0, cw, cb, dtb, dtbt, alog, alogt, dsum, ng, dims):
    off_z, off_x, off_b, off_c, inner, groups, nstate = dims
    gw = inner // groups
    hg = dtgt.shape[2]
    q = SSD_CHUNK
    nc = t_len // q
    rb = row_off // t_len
    cbk, cck = (off_b - off_x) // nstate, (off_c - off_x) // nstate
    return pl.pallas_call(
        _ssd_kernel,
        out_shape=(SDS((nb * t_len, inner), BF16), SDS((nb, 2, groups, gw, nstate), F32)),
        grid=(nb, groups),
        in_specs=[
            pl.BlockSpec((t_len, gw), lambda b, g: (rb + b, off_x // gw + g)),
            pl.BlockSpec((t_len, nstate), lambda b, g: (rb + b, off_b // nstate + g)),
            pl.BlockSpec((t_len, nstate), lambda b, g: (rb + b, off_c // nstate + g)),
            pl.BlockSpec((t_len, gw), lambda b, g: (rb + b, off_z // gw + g)),
            pl.BlockSpec((t_len, LANES), lambda b, g: (rb + b, g)),
            pl.BlockSpec((None, nc, hg, q), lambda b, g: (g, rb + b, 0, 0)),
            pl.BlockSpec((4, gw), lambda b, g: (0, g)),
            pl.BlockSpec((4, nstate), lambda b, g: (0, cbk + g)),
            pl.BlockSpec((4, nstate), lambda b, g: (0, cck + g)),
            pl.BlockSpec((1, gw), lambda b, g: (0, g)),
            pl.BlockSpec((1, nstate), lambda b, g: (0, cbk + g)),
            pl.BlockSpec((1, nstate), lambda b, g: (0, cck + g)),
            pl.BlockSpec((None, 2, 1, LANES), lambda b, g: (g, 0, 0, 0)),
            pl.BlockSpec((None, 2, hg, 1), lambda b, g: (g, 0, 0, 0)),
            pl.BlockSpec((None, 2, 1, LANES), lambda b, g: (g, 0, 0, 0)),
            pl.BlockSpec((None, 2, hg, 1), lambda b, g: (g, 0, 0, 0)),
            pl.BlockSpec((None, 1, gw), lambda b, g: (g, 0, 0)),
            pl.BlockSpec((1, gw), lambda b, g: (0, g)),
            pl.BlockSpec((None, 2, None, gw, nstate), lambda b, g: (b, 0, g, 0, 0)),
        ],
        out_specs=(
            pl.BlockSpec((t_len, gw), lambda b, g: (b, g)),
            pl.BlockSpec((None, 2, None, gw, nstate), lambda b, g: (b, 0, g, 0, 0)),
        ),
        scratch_shapes=[
            pltpu.VMEM((t_len, gw), F32),
            pltpu.VMEM((t_len, nstate), BF16),
            pltpu.VMEM((t_len, nstate), BF16),
            pltpu.VMEM((t_len, gw), F32),
            pltpu.VMEM((t_len, LANES), F32),
            pltpu.VMEM((t_len, gw), F32),
            pltpu.VMEM((nstate, gw), F32),
        ],
        compiler_params=_params("arbitrary", "arbitrary"),
        name="ssd",
    )(proj, proj, proj, proj, dt_pad, dtgt, cw, cw, cw, cb, cb, cb, dtb, dtbt, alog, alogt, dsum, ng, s0)


def _router_kernel(x_ref, g_ref, sc_ref, sh_ref, rw_ref, rb_ref,
                   h_ref, w_ref, i_ref, r_ref, cnt_ref, carry_s, *, n_experts):
    tm = x_ref.shape[0]

    @pl.when(pl.program_id(0) == 0)
    def _():
        carry_s[...] = jnp.zeros_like(carry_s)

    h = _norm_mod(x_ref[...], g_ref[...], sc_ref[...], sh_ref[...])
    h_ref[...] = _pack_halves(h)
    rw = rw_ref[...]
    rw_hi = rw.astype(BF16)
    rw_lo = (rw - rw_hi.astype(F32)).astype(BF16)
    h_hi = h.astype(BF16)
    h_lo = (h - h_hi.astype(F32)).astype(BF16)
    logits = _dot(h_hi, rw_hi) + _dot(h_lo, rw_hi) + _dot(h_hi, rw_lo) + rb_ref[...]
    lane = lax.broadcasted_iota(I32, (tm, LANES), 1)
    cur = jnp.where(lane < n_experts, logits, -jnp.inf)
    vals, idxs = [], []
    member = jnp.zeros((tm, LANES), F32)
    for _k in range(TOP_K):
        m = jnp.max(cur, axis=-1, keepdims=True)
        idx = jnp.min(jnp.where(cur == m, lane, LANES), axis=-1, keepdims=True)
        sel = lane == idx
        vals.append(m)
        idxs.append(idx)
        member = member + sel.astype(F32)
        cur = jnp.where(sel, -jnp.inf, cur)
    exps = [jnp.exp(v - vals[0]) for v in vals]
    denom = exps[0]
    for e in exps[1:]:
        denom = denom + e
    r_i = lax.broadcasted_iota(I32, (tm, tm), 0)
    c_i = lax.broadcasted_iota(I32, (tm, tm), 1)
    earlier = (c_i < r_i).astype(BF16)
    before = _dot(earlier, member.astype(BF16)) + carry_s[...]
    w_out = jnp.zeros((tm, LANES), F32)
    i_out = jnp.zeros((tm, LANES), I32)
    r_out = jnp.zeros((tm, LANES), I32)
    for k in range(TOP_K):
        rank = jnp.sum(jnp.where(lane == idxs[k], before, 0.0), axis=-1, keepdims=True)
        w_out = jnp.where(lane == k, exps[k] / denom, w_out)
        i_out = jnp.where(lane == k, idxs[k], i_out)
        r_out = jnp.where(lane == k, rank.astype(I32), r_out)
    w_ref[...] = w_out
    i_ref[...] = i_out
    r_ref[...] = r_out
    carry_s[...] = carry_s[...] + jnp.sum(member, axis=0, keepdims=True)
    cnt_ref[...] = jnp.broadcast_to(carry_s[...], cnt_ref.shape)


def _router(x, g, modr, sh_idx, sc_idx, cond_of_tile, rw_pad, rb_pad, n_experts, tm):
    n, d = x.shape
    return pl.pallas_call(
        functools.partial(_router_kernel, n_experts=n_experts),
        out_shape=(SDS((n, d // 2), U32), SDS((n, LANES), F32), SDS((n, LANES), I32),
                   SDS((n, LANES), I32), SDS((8, LANES), F32)),
        grid=(n // tm,),
        in_specs=[
            pl.BlockSpec((tm, d), lambda i: (i, 0)),
            pl.BlockSpec((1, d), lambda i: (0, 0)),
            pl.BlockSpec((None, None, 1, d), lambda i: (cond_of_tile(i, tm), sc_idx, 0, 0)),
            pl.BlockSpec((None, None, 1, d), lambda i: (cond_of_tile(i, tm), sh_idx, 0, 0)),
            pl.BlockSpec((d, LANES), lambda i: (0, 0)),
            pl.BlockSpec((1, LANES), lambda i: (0, 0)),
        ],
        out_specs=(
            pl.BlockSpec((tm, d // 2), lambda i: (i, 0)),
            pl.BlockSpec((tm, LANES), lambda i: (i, 0)),
            pl.BlockSpec((tm, LANES), lambda i: (i, 0)),
            pl.BlockSpec((tm, LANES), lambda i: (i, 0)),
            pl.BlockSpec((8, LANES), lambda i: (0, 0)),
        ),
        scratch_shapes=[pltpu.VMEM((1, LANES), F32)],
        compiler_params=_params("arbitrary"),
        name="router",
    )(x, g.reshape(1, d), modr, modr, rw_pad, rb_pad)


def _row_copy(src_hbm, dst, sem, src_row, dst_row):
    return pltpu.make_async_copy(src_hbm.at[pl.ds(src_row, 1), :], dst.at[pl.ds(dst_row, 1), :], sem)


def _dispatch_kernel(src_ref, nv_ref, h_hbm, o_ref, buf, sem):
    tm = o_ref.shape[0]
    i = pl.program_id(0)
    nv = nv_ref[0]

    def issue(tile, slot):
        base = tile * tm

        def body(r, c):
            _row_copy(h_hbm, buf.at[slot], sem.at[slot], src_ref[base + r], r).start()
            return c

        lax.fori_loop(0, tm, body, 0, unroll=8)

    def drain(slot):
        pltpu.make_async_copy(h_hbm.at[pl.ds(0, tm), :], buf.at[slot], sem.at[slot]).wait()

    @pl.when(jnp.logical_and(i == 0, nv > 0))
    def _():
        issue(0, 0)

    @pl.when(i + 1 < nv)
    def _():
        issue(i + 1, (i + 1) & 1)

    @pl.when(i < nv)
    def _():
        slot = i & 1
        drain(slot)
        half = buf.shape[2]
        lo, hi = _unpack_halves(buf[slot])
        o_ref[:, :half] = lo.astype(o_ref.dtype)
        o_ref[:, half:] = hi.astype(o_ref.dtype)

    @pl.when(i >= nv)
    def _():
        o_ref[...] = jnp.zeros_like(o_ref)


def _dispatch(h_packed, src_tok, n_valid, n_tiles, tm):
    half = h_packed.shape[1]
    d = 2 * half
    return pl.pallas_call(
        _dispatch_kernel,
        out_shape=SDS((n_tiles * tm, d), BF16),
        grid_spec=pltpu.PrefetchScalarGridSpec(
            num_scalar_prefetch=2,
            grid=(n_tiles,),
            in_specs=[pl.BlockSpec(memory_space=pl.ANY)],
            out_specs=pl.BlockSpec((tm, d), lambda i, s, nv: (i, 0)),
            scratch_shapes=[pltpu.VMEM((2, tm, half), U32), pltpu.SemaphoreType.DMA((2,))],
        ),
        compiler_params=_params("arbitrary"),
        name="moe_dispatch",
    )(src_tok, n_valid, h_packed)


def _weight_copy(w_hbm, buf, sem, layer, expert, col_blk, slot, tn):
    cols = pl.ds(pl.multiple_of(col_blk * tn, tn), tn)
    return pltpu.make_async_copy(w_hbm.at[layer, expert, :, cols], buf.at[slot], sem.at[slot])


def _group_weights(te_ref, first_ref, nxt_ref, gidx_ref, meta_ref, weights, layer, tn):
    j = pl.program_id(0)
    i = pl.program_id(1)
    nj = pl.num_programs(0)
    ng = meta_ref[1]

    @pl.when(first_ref[i] == 1)
    def _():
        slot = (gidx_ref[i] + j * ng) & 1

        @pl.when(jnp.logical_and(i == 0, j == 0))
        def _():
            for w_hbm, buf, _bf, sem in weights:
                _weight_copy(w_hbm, buf, sem, layer, te_ref[i], j, slot, tn).start()

        for w_hbm, buf, _bf, sem in weights:
            _weight_copy(w_hbm, buf, sem, layer, te_ref[i], j, slot, tn).wait()

        next_j = jnp.where(gidx_ref[i] == ng - 1, j + 1, j)

        @pl.when(next_j < nj)
        def _():
            for w_hbm, buf, _bf, sem in weights:
                _weight_copy(w_hbm, buf, sem, layer, nxt_ref[i], next_j, 1 - slot, tn).start()

        for _w, buf, bf, _sem in weights:
            k = bf.shape[0]
            rows_per = min(CAST_ROWS, k)

            def cast(c, carry, buf=buf, bf=bf, rows_per=rows_per):
                rows = pl.ds(pl.multiple_of(c * rows_per, rows_per), rows_per)
                bf[rows, :] = buf[slot, rows, :].astype(BF16)
                return carry

            lax.fori_loop(0, k // rows_per, cast, 0)


def _gmm_up_kernel(te_ref, first_ref, nxt_ref, gidx_ref, tb_ref, meta_ref,
                   x_ref, wg_hbm, wu_hbm, bg_ref, bu_ref, o_ref,
                   wg_buf, wu_buf, wg_bf, wu_bf, sem_g, sem_u, *, layer):
    i = pl.program_id(1)
    tn = o_ref.shape[1]

    @pl.when(i < meta_ref[0])
    def _():
        _group_weights(te_ref, first_ref, nxt_ref, gidx_ref, meta_ref,
                       [(wg_hbm, wg_buf, wg_bf, sem_g), (wu_hbm, wu_buf, wu_bf, sem_u)], layer, tn)
        x = x_ref[...]
        g = jnp.minimum(_dot(x, wg_bf[...]) + bg_ref[...], SWIGLU_LIMIT)
        u = jnp.clip(_dot(x, wu_bf[...]) + bu_ref[...], -SWIGLU_LIMIT, SWIGLU_LIMIT)
        o_ref[...] = ((u + 1.0) * g * jax.nn.sigmoid(SWIGLU_ALPHA * g)).astype(o_ref.dtype)

    @pl.when(i >= meta_ref[0])
    def _():
        o_ref[...] = jnp.zeros_like(o_ref)


def _gmm_up(xs, wg, wu, bg, bu, plan, layer, tm, tn):
    ns, d = xs.shape
    depth, n_exp, _, de = wg.shape
    n_tiles = ns // tm
    eoff = layer * n_exp
    return pl.pallas_call(
        functools.partial(_gmm_up_kernel, layer=layer),
        out_shape=SDS((ns, de), BF16),
        grid_spec=pltpu.PrefetchScalarGridSpec(
            num_scalar_prefetch=6,
            grid=(de // tn, n_tiles),
            in_specs=[
                pl.BlockSpec((tm, d), lambda j, i, te, fi, nx, gi, tb, mt: (tb[i], 0)),
                pl.BlockSpec(memory_space=pl.ANY),
                pl.BlockSpec(memory_space=pl.ANY),
                pl.BlockSpec((None, 1, tn), lambda j, i, te, fi, nx, gi, tb, mt: (eoff + te[i], 0, j)),
                pl.BlockSpec((None, 1, tn), lambda j, i, te, fi, nx, gi, tb, mt: (eoff + te[i], 0, j)),
            ],
            out_specs=pl.BlockSpec((tm, tn), lambda j, i, te, fi, nx, gi, tb, mt: (i, j)),
            scratch_shapes=[
                pltpu.VMEM((2, d, tn), F32), pltpu.VMEM((2, d, tn), F32),
                pltpu.VMEM((d, tn), BF16), pltpu.VMEM((d, tn), BF16),
                pltpu.SemaphoreType.DMA((2,)), pltpu.SemaphoreType.DMA((2,)),
            ],
        ),
        compiler_params=_params("arbitrary", "arbitrary"),
        name="moe_up",
    )(*plan, xs, wg, wu, bg.reshape(depth * n_exp, 1, de), bu.reshape(depth * n_exp, 1, de))


def _gmm_down_kernel(te_ref, first_ref, nxt_ref, gidx_ref, tb_ref, meta_ref,
                     a_ref, w_hbm, b_ref, o_ref, w_buf, w_bf, sem, *, layer):
    i = pl.program_id(1)
    tn = w_bf.shape[1]

    @pl.when(i < meta_ref[0])
    def _():
        _group_weights(te_ref, first_ref, nxt_ref, gidx_ref, meta_ref, [(w_hbm, w_buf, w_bf, sem)], layer, tn)
        o_ref[...] = _pack_halves(_dot(a_ref[...], w_bf[...]) + b_ref[...])

    @pl.when(i >= meta_ref[0])
    def _():
        o_ref[...] = jnp.zeros_like(o_ref)


def _gmm_down(act, wd, bd, plan, layer, tm, tn):
    ns, de = act.shape
    depth, n_exp, _, d = wd.shape
    n_tiles = ns // tm
    eoff = layer * n_exp
    return pl.pallas_call(
        functools.partial(_gmm_down_kernel, layer=layer),
        out_shape=SDS((ns, d // 2), U32),
        grid_spec=pltpu.PrefetchScalarGridSpec(
            num_scalar_prefetch=6,
            grid=(d // tn, n_tiles),
            in_specs=[
                pl.BlockSpec((tm, de), lambda j, i, te, fi, nx, gi, tb, mt: (tb[i], 0)),
                pl.BlockSpec(memory_space=pl.ANY),
                pl.BlockSpec((None, 1, tn), lambda j, i, te, fi, nx, gi, tb, mt: (eoff + te[i], 0, j)),
            ],
            out_specs=pl.BlockSpec((tm, tn // 2), lambda j, i, te, fi, nx, gi, tb, mt: (i, j)),
            scratch_shapes=[
                pltpu.VMEM((2, de, tn), F32), pltpu.VMEM((de, tn), BF16), pltpu.SemaphoreType.DMA((2,)),
            ],
        ),
        compiler_params=_params("arbitrary", "arbitrary"),
        name="moe_down",
    )(*plan, act, wd, bd.reshape(depth * n_exp, 1, d))


def _combine_kernel(pos_ref, y_hbm, x_ref, w_ref, g_ref, fg_ref, *rest, final_norm, n_first, col_blk):
    outs, (buf, xs_s, sem) = rest[:-3], rest[-3:]
    tm = x_ref.shape[0]
    i = pl.program_id(0)
    n_steps = pl.num_programs(0)

    def issue(tile, slot):
        base = tile * tm * TOP_K

        def body(r, c):
            for k in range(TOP_K):
                _row_copy(y_hbm, buf.at[slot, k], sem.at[slot], pos_ref[base + r * TOP_K + k], r).start()
            return c

        lax.fori_loop(0, tm, body, 0, unroll=4)

    def drain(slot):
        for k in range(TOP_K):
            pltpu.make_async_copy(y_hbm.at[pl.ds(0, tm), :], buf.at[slot, k], sem.at[slot]).wait()

    @pl.when(i == 0)
    def _():
        issue(0, 0)

    @pl.when(i + 1 < n_steps)
    def _():
        issue(i + 1, (i + 1) & 1)

    slot = i & 1
    drain(slot)
    w = w_ref[...]
    half_blk = col_blk // 2
    sub = min(half_blk, 2 * LANES)
    dst = xs_s if final_norm else outs[0]
    for jb in range(x_ref.shape[1] // col_blk):
        for s0 in range(0, half_blk, sub):
            cols = slice(jb * half_blk + s0, jb * half_blk + s0 + sub)
            acc_lo = acc_hi = None
            for k in range(TOP_K):
                lo, hi = _unpack_halves(buf[slot, k, :, cols])
                wk = w[:, k:k + 1]
                acc_lo = wk * lo if acc_lo is None else acc_lo + wk * lo
                acc_hi = wk * hi if acc_hi is None else acc_hi + wk * hi
            for part, acc in ((0, acc_lo), (half_blk, acc_hi)):
                oc = slice(jb * col_blk + part + s0, jb * col_blk + part + s0 + sub)
                dst[:, oc] = x_ref[:, oc] + g_ref[:, oc] * acc
    if final_norm:
        x = xs_s[...]
        ms = jnp.mean(x * x, axis=-1, keepdims=True)
        x = x * lax.rsqrt(ms + EPS) * fg_ref[...]
        first_ref, rest_ref = outs

        @pl.when(i < n_first)
        def _():
            first_ref[...] = x

        @pl.when(i >= n_first)
        def _():
            rest_ref[...] = x


def _combine(y_sorted, pos_flat, x, w_pad, modr, g_idx, cond_of_tile, final_g, final_norm, n_first_rows,
             col_blk, tm):
    n, d = x.shape
    n_first = n_first_rows // tm
    if final_norm:
        out_shape = (SDS((n_first_rows, d), F32), SDS((n - n_first_rows, d), F32))
        out_specs = (pl.BlockSpec((tm, d), lambda i, p: (jnp.minimum(i, n_first - 1), 0)),
                     pl.BlockSpec((tm, d), lambda i, p: (jnp.maximum(i - n_first, 0), 0)))
    else:
        out_shape = SDS((n, d), F32)
        out_specs = pl.BlockSpec((tm, d), lambda i, p: (i, 0))
    return pl.pallas_call(
        functools.partial(_combine_kernel, final_norm=final_norm, n_first=n_first, col_blk=col_blk),
        out_shape=out_shape,
        grid_spec=pltpu.PrefetchScalarGridSpec(
            num_scalar_prefetch=1,
            grid=(n // tm,),
            in_specs=[
                pl.BlockSpec(memory_space=pl.ANY),
                pl.BlockSpec((tm, d), lambda i, p: (i, 0)),
                pl.BlockSpec((tm, LANES), lambda i, p: (i, 0)),
                pl.BlockSpec((None, None, 1, d), lambda i, p: (cond_of_tile(i, tm), g_idx, 0, 0)),
                pl.BlockSpec((1, d), lambda i, p: (0, 0)),
            ],
            out_specs=out_specs,
            scratch_shapes=[pltpu.VMEM((2, TOP_K, tm, d // 2), U32), pltpu.VMEM((tm, d), F32),
                            pltpu.SemaphoreType.DMA((2,))],
        ),
        compiler_params=_params("arbitrary"),
        name="moe_combine",
    )(pos_flat, y_sorted, x, w_pad, modr, final_g.reshape(1, d))


def _pick_tile(n, pref):
    t = min(pref, n)
    while n % t:
        t //= 2
    return t


def _routing_plan(counts, n_tiles, tm):
    n_experts = counts.shape[0]
    padded = (counts + tm - 1) // tm * tm
    ends = jnp.cumsum(padded)
    starts = ends - padded
    n_valid = ends[-1] // tm
    tile_ids = jnp.arange(n_tiles, dtype=I32)
    tile_b = jnp.minimum(tile_ids, jnp.maximum(n_valid - 1, 0))
    tile_e = jnp.minimum(jnp.sum(ends[None, :] <= (tile_b * tm)[:, None], axis=1), n_experts - 1).astype(I32)
    prev_e = jnp.concatenate([jnp.full((1,), -1, I32), tile_e[:-1]])
    first = ((tile_e != prev_e) & (tile_ids < n_valid)).astype(I32)
    nonempty = padded > 0
    gidx_of_e = jnp.cumsum(nonempty.astype(I32)) - 1
    n_groups = jnp.sum(nonempty.astype(I32))
    big = n_experts
    idx = jnp.where(nonempty, jnp.arange(n_experts, dtype=I32), big)
    at_or_after = lax.cummin(idx, reverse=True)
    after = jnp.concatenate([at_or_after[1:], jnp.full((1,), big, I32)])
    nxt_of_e = jnp.where(after == big, at_or_after[0], after)
    nxt_of_e = jnp.minimum(nxt_of_e, n_experts - 1).astype(I32)
    meta = jnp.stack([n_valid, n_groups]).astype(I32)
    plan = (tile_e, first, nxt_of_e[tile_e], gidx_of_e[tile_e].astype(I32), tile_b.astype(I32), meta)
    return starts, plan


def kernel(x_prompt, x_sample, c, state_lru, state_ssd, c_ctx, ada_w, ada_b, norm1_g, norm2_g, w_in, lru_conv_w, lru_conv_b, lru_wa, lru_ba, lru_wi, lru_bi, lru_lambda, ssd_conv_w, ssd_conv_b, ssd_dt_bias, ssd_a_log, ssd_d, ssd_norm_g, w_out, router_w, router_b, exp_w_gate, exp_b_gate, exp_w_up, exp_b_up, exp_w_down, exp_b_down, final_norm_g):
    b_ctx, t_ctx, d = x_prompt.shape
    b_lat, t_lat, _ = x_sample.shape
    depth = ada_w.shape[0]
    n_ctx, n_lat = b_ctx * t_ctx, b_lat * t_lat
    n = n_ctx + n_lat
    lru_w = lru_conv_w.shape[-1]
    _, _, heads, hp, nstate = state_ssd.shape[1:]
    inner = heads * hp
    conv_ch = ssd_conv_w.shape[-1]
    groups = (conv_ch - inner) // (2 * nstate)
    gw = inner // groups
    hg = heads // groups
    n_experts = router_w.shape[-1]
    lru_blk = lru_wa.shape[3]
    assert lru_blk == LANES and hg <= 8 and LANES % hp == 0 and n_experts <= LANES
    assert t_ctx % SSD_CHUNK == 0 and t_lat % SSD_CHUNK == 0 and n_ctx % t_lat == 0
    assert b_lat + 1 <= 8

    off_z = 2 * lru_w
    off_x = off_z + inner
    off_b = off_x + inner
    off_c = off_b + groups * nstate
    n_main = off_c + groups * nstate
    ssd_dims = (off_z, off_x, off_b, off_c, inner, groups, nstate)

    tm = _pick_tile(min(t_ctx, t_lat), 512)
    cond_of_tile = functools.partial(_cond_row, n_ctx=n_ctx, t_lat=t_lat)

    xa, xb = x_prompt.reshape(n_ctx, d), x_sample.reshape(n_lat, d)
    cond8 = jnp.zeros((8, d), F32).at[0].set(c_ctx).at[1:1 + b_lat].set(c)
    mod = _ada_mod(cond8, ada_w, ada_b)
    modr_all = mod[:, :1 + b_lat].reshape(depth, 1 + b_lat, 6, 1, d)

    tn_in = _pick_tile(n_main, 1024)
    tn_out = _pick_tile(d, 1024)
    tm_mm = _pick_tile(n, 1024)
    tm_out = _pick_tile(min(n_ctx, t_lat), 512)
    tn_down = _pick_tile(d, MOE_DOWN_COLS)
    moe_tm = MOE_TILE
    n_tiles = (n * TOP_K + n_experts * (moe_tm - 1)) // moe_tm + 1
    tok_ids = jnp.repeat(jnp.arange(n, dtype=I32), TOP_K)
    rw_pad = jnp.zeros((depth, d, LANES), F32).at[:, :, :n_experts].set(router_w)
    rb_pad = jnp.zeros((depth, 1, LANES), F32).at[:, 0, :n_experts].set(router_b)

    lru_states, ssd_states = [], []
    zeros_lru = jnp.zeros((b_ctx, 2, lru_w), F32)
    zeros_ssd = jnp.zeros((b_ctx, 2, groups, gw, nstate), F32)
    y_final = None
    for l in range(depth):
        modr = modr_all[l]
        h = _normmod(xa, xb, norm1_g[l], modr, 0, 1, cond_of_tile, tm)
        w_in_bf = w_in[l].astype(BF16)
        proj = _matmul(h, w_in_bf[:, :n_main], tm_mm, tn_in)
        w_dt = jnp.zeros((d, groups, LANES), BF16).at[:, :, :hg].set(
            w_in_bf[:, n_main:].reshape(d, groups, hg)).reshape(d, groups * LANES)
        dt_pad = _matmul(h, w_dt, tm_mm, _pick_tile(groups * LANES, 512))
        dtgt = dt_pad.reshape(n // SSD_CHUNK, SSD_CHUNK, groups, LANES)[..., :hg].transpose(2, 0, 3, 1)

        wg = jnp.concatenate([lru_wa[l, 0], lru_wi[l, 0], lru_wa[l, 1], lru_wi[l, 1]], axis=-1).astype(BF16)
        gb = jnp.stack([lru_ba[l, 0], lru_bi[l, 0], lru_ba[l, 1], lru_bi[l, 1]], axis=0)
        lru_args = (lru_conv_w[l], lru_conv_b[l].reshape(1, lru_w), wg, gb, lru_lambda[l])
        y_lru_c, h_c = _lru_group(proj, 0, b_ctx, t_ctx, zeros_lru, *lru_args)
        y_lru_s, _ = _lru_group(proj, n_ctx, b_lat, t_lat, state_lru[:, l], *lru_args)

        def per_group(p, fill):
            pg = p.reshape(2, groups, hg).transpose(1, 0, 2)
            rowf = jnp.full((groups, 2, 1, LANES), fill, F32).at[:, :, 0, :hg].set(pg)
            return rowf, pg.reshape(groups, 2, hg, 1)

        dtb, dtbt = per_group(ssd_dt_bias[l], 0.0)
        alog, alogt = per_group(ssd_a_log[l], 0.0)
        dsum = jnp.repeat((ssd_d[l, 0] + ssd_d[l, 1]).reshape(groups, 1, hg), hp, axis=-1)
        ssd_args = (ssd_conv_w[l], ssd_conv_b[l].reshape(1, conv_ch), dtb, dtbt, alog, alogt, dsum,
                    ssd_norm_g[l].reshape(1, inner), ssd_dims)
        y_ssd_c, s_c = _ssd_group(proj, dt_pad, dtgt, 0, b_ctx, t_ctx, zeros_ssd, *ssd_args)
        s0_lat = state_ssd[:, l].reshape(b_lat, 2, groups, gw, nstate)
        y_ssd_s, _ = _ssd_group(proj, dt_pad, dtgt, n_ctx, b_lat, t_lat, s0_lat, *ssd_args)
        lru_states.append(h_c)
        ssd_states.append(s_c.reshape(b_ctx, 2, heads, hp, nstate))

        y_lru = jnp.concatenate([y_lru_c, y_lru_s], axis=0)
        y_ssd = jnp.concatenate([y_ssd_c, y_ssd_s], axis=0)
        x = _wout(y_lru, y_ssd, w_out[l].astype(BF16), xa, xb, modr, 2, cond_of_tile, tm_out, tn_out)

        rt = _pick_tile(tm, 256)
        h2, w_pad, idx_pad, rank_pad, cnt = _router(x, norm2_g[l], modr, 3, 4, cond_of_tile,
                                                     rw_pad[l], rb_pad[l], n_experts, rt)
        starts, plan = _routing_plan(cnt[0, :n_experts].astype(I32), n_tiles, moe_tm)
        pos = (starts[idx_pad[:, :TOP_K]] + rank_pad[:, :TOP_K]).reshape(-1)
        src_tok = jnp.zeros((n_tiles * moe_tm,), I32).at[pos].set(tok_ids)
        n_valid = plan[-1][:1]

        xs = _dispatch(h2, src_tok, n_valid, n_tiles, moe_tm)
        act = _gmm_up(xs, exp_w_gate, exp_w_up, exp_b_gate, exp_b_up, plan, l, moe_tm,
                      _pick_tile(exp_w_gate.shape[-1], MOE_UP_COLS))
        y_sorted = _gmm_down(act, exp_w_down, exp_b_down, plan, l, moe_tm, tn_down)
        last = l == depth - 1
        out = _combine(y_sorted, pos, x, w_pad, modr, 5, cond_of_tile, final_norm_g, last, n_ctx,
                       tn_down, _pick_tile(tm, COMBINE_TILE))
        if last:
            y_final = out
        else:
            xa = xb = out

    y_prompt = y_final[0].reshape(b_ctx, t_ctx, d)
    y_sample = y_final[1].reshape(b_lat, t_lat, d)
    return (y_prompt, y_sample, jnp.stack(lru_states, axis=1), jnp.stack(ssd_states, axis=1))
```

```python
import functools

import jax
import jax.numpy as jnp
from jax import lax
from jax.experimental import pallas as pl
from jax.experimental.pallas import tpu as pltpu

F32 = jnp.float32
BF16 = jnp.bfloat16
I32 = jnp.int32
U32 = jnp.uint32
HIGHEST = lax.Precision.HIGHEST

TOP_K = 4
LRU_C = 8.0
SWIGLU_LIMIT = 7.0
SWIGLU_ALPHA = 1.702
EPS = 1e-6
SSD_CHUNK = 128
LANES = 128
NEG_BIG = -1e30
VMEM_LIMIT_BYTES = 56 * 1024 * 1024
MOE_TILE = 256
COMBINE_TILE = 256
LRU_SCRATCH_BYTES = 2 * 1024 * 1024
CAST_ROWS = 512
MOE_UP_COLS = 512
MOE_DOWN_COLS = 2048

SDS = jax.ShapeDtypeStruct


def _params(*sem):
    return pltpu.CompilerParams(dimension_semantics=sem, vmem_limit_bytes=VMEM_LIMIT_BYTES)


def _dot(a, b):
    return jnp.dot(a, b, preferred_element_type=F32)


def _dot_exact(a, b):
    return jnp.dot(a, b, precision=HIGHEST, preferred_element_type=F32)


def _split3(x):
    hi = x.astype(BF16)
    r1 = x - hi.astype(F32)
    mid = r1.astype(BF16)
    lo = (r1 - mid.astype(F32)).astype(BF16)
    return hi, mid, lo


def _dot_sel_right(x, sel):
    hi, mid, lo = _split3(x)
    return _dot(hi, sel) + _dot(mid, sel) + _dot(lo, sel)


def _dot_sel_left(sel, x):
    hi, mid, lo = _split3(x)
    return _dot(sel, hi) + _dot(sel, mid) + _dot(sel, lo)


def _pack_halves(x):
    c = x.shape[1] // 2
    lo = pltpu.bitcast(x[:, :c].astype(BF16).astype(F32), U32)
    hi = pltpu.bitcast(x[:, c:].astype(BF16).astype(F32), U32)
    return lax.shift_right_logical(lo, jnp.uint32(16)) | hi


def _unpack_halves(w):
    lo = pltpu.bitcast(lax.shift_left(w, jnp.uint32(16)), F32)
    hi = pltpu.bitcast(w & jnp.uint32(0xFFFF0000), F32)
    return lo, hi


def _softplus(x):
    return jnp.maximum(x, 0.0) + jnp.log1p(jnp.exp(-jnp.abs(x)))


def _silu(x):
    return x * jax.nn.sigmoid(x)


def _cond_row(i, tm, n_ctx, t_lat):
    r = i * tm
    return jnp.where(r < n_ctx, 0, 1 + (r - n_ctx) // t_lat)


def _ada_kernel(c_ref, w_ref, b_ref, o_ref):
    c = c_ref[...]
    o_ref[...] = _dot_exact(_silu(c), w_ref[...]) + b_ref[...]


def _ada_mod(cond8, ada_w, ada_b):
    depth, d, d6 = ada_w.shape
    tn = min(512, d6)
    return pl.pallas_call(
        _ada_kernel,
        out_shape=SDS((depth, 8, d6), F32),
        grid=(depth, d6 // tn),
        in_specs=[
            pl.BlockSpec((8, d), lambda l, j: (0, 0)),
            pl.BlockSpec((None, d, tn), lambda l, j: (l, 0, j)),
            pl.BlockSpec((None, 1, tn), lambda l, j: (l, 0, j)),
        ],
        out_specs=pl.BlockSpec((None, 8, tn), lambda l, j: (l, 0, j)),
        compiler_params=_params("arbitrary", "arbitrary"),
        name="ada_mod",
    )(cond8, ada_w, ada_b.reshape(depth, 1, d6))


def _norm_mod(x, g, sc, sh):
    ms = jnp.mean(x * x, axis=-1, keepdims=True)
    return (x * lax.rsqrt(ms + EPS) * g) * (1.0 + sc) + sh


def _pick_part(i, n_first, xa_ref, xb_ref):
    return jnp.where(i < n_first, xa_ref[...], xb_ref[...])


def _part_maps(n_first, col):
    return (lambda *a: (jnp.minimum(a[-1], n_first - 1), col(*a)),
            lambda *a: (jnp.maximum(a[-1] - n_first, 0), col(*a)))


def _normmod_kernel(xa_ref, xb_ref, g_ref, sc_ref, sh_ref, o_ref, *, n_first):
    x = _pick_part(pl.program_id(0), n_first, xa_ref, xb_ref)
    o_ref[...] = _norm_mod(x, g_ref[...], sc_ref[...], sh_ref[...]).astype(o_ref.dtype)


def _normmod(xa, xb, g, modr, sh_idx, sc_idx, cond_of_tile, tm):
    d = xa.shape[1]
    n = xa.shape[0] if xa is xb else xa.shape[0] + xb.shape[0]
    n_first = xa.shape[0] // tm
    map_a, map_b = _part_maps(n_first, lambda i: 0)
    return pl.pallas_call(
        functools.partial(_normmod_kernel, n_first=n_first),
        out_shape=SDS((n, d), BF16),
        grid=(n // tm,),
        in_specs=[
            pl.BlockSpec((tm, d), map_a),
            pl.BlockSpec((tm, d), map_b),
            pl.BlockSpec((1, d), lambda i: (0, 0)),
            pl.BlockSpec((None, None, 1, d), lambda i: (cond_of_tile(i, tm), sc_idx, 0, 0)),
            pl.BlockSpec((None, None, 1, d), lambda i: (cond_of_tile(i, tm), sh_idx, 0, 0)),
        ],
        out_specs=pl.BlockSpec((tm, d), lambda i: (i, 0)),
        compiler_params=_params("arbitrary"),
        name="norm_mod",
    )(xa, xb, g.reshape(1, d), modr, modr)


def _mm_kernel(a_ref, w_ref, o_ref):
    o_ref[...] = _dot(a_ref[...], w_ref[...]).astype(o_ref.dtype)


def _matmul(a, w, tm, tn, out_dtype=F32):
    n, k = a.shape
    m = w.shape[1]
    return pl.pallas_call(
        _mm_kernel,
        out_shape=SDS((n, m), out_dtype),
        grid=(m // tn, n // tm),
        in_specs=[
            pl.BlockSpec((tm, k), lambda j, i: (i, 0)),
            pl.BlockSpec((k, tn), lambda j, i: (0, j)),
        ],
        out_specs=pl.BlockSpec((tm, tn), lambda j, i: (i, j)),
        compiler_params=_params("arbitrary", "arbitrary"),
        name="matmul",
    )(a, w)


def _wout_kernel(a1_ref, a2_ref, w1_ref, w2_ref, xa_ref, xb_ref, g_ref, o_ref, *, n_first):
    m = _dot(a1_ref[...], w1_ref[...]) + _dot(a2_ref[...], w2_ref[...])
    o_ref[...] = _pick_part(pl.program_id(1), n_first, xa_ref, xb_ref) + g_ref[...] * m


def _wout(y1, y2, w, xa, xb, modr, g_idx, cond_of_tile, tm, tn):
    n, k1 = y1.shape
    k2 = y2.shape[1]
    d = w.shape[1]
    assert k1 == k2
    n_first = xa.shape[0] // tm
    map_a, map_b = _part_maps(n_first, lambda j, i: j)
    return pl.pallas_call(
        functools.partial(_wout_kernel, n_first=n_first),
        out_shape=SDS((n, d), F32),
        grid=(d // tn, n // tm),
        in_specs=[
            pl.BlockSpec((tm, k1), lambda j, i: (i, 0)),
            pl.BlockSpec((tm, k2), lambda j, i: (i, 0)),
            pl.BlockSpec((k1, tn), lambda j, i: (0, j)),
            pl.BlockSpec((k2, tn), lambda j, i: (1, j)),
            pl.BlockSpec((tm, tn), map_a),
            pl.BlockSpec((tm, tn), map_b),
            pl.BlockSpec((None, None, 1, tn), lambda j, i: (cond_of_tile(i, tm), g_idx, 0, j)),
        ],
        out_specs=pl.BlockSpec((tm, tn), lambda j, i: (i, j)),
        compiler_params=_params("arbitrary", "arbitrary"),
        name="w_out",
    )(y1, y2, w, w, xa, xb, modr)


def _shift_rows(x, s):
    t_len = x.shape[0]
    rolled = pltpu.roll(x, s % t_len, axis=0)
    t = lax.broadcasted_iota(I32, x.shape, 0)
    keep = (t >= s) if s > 0 else (t < t_len + s)
    return jnp.where(keep, rolled, 0.0)


def _dwconv(x, w_ref, b_ref):
    return (b_ref[...] + w_ref[2:3, :] * x + w_ref[1:2, :] * _shift_rows(x, 1)
            + w_ref[0:1, :] * _shift_rows(x, 2) + w_ref[3:4, :] * _shift_rows(x, -1))


def _lru_kernel(x_ref, g_ref, cw_ref, cb_ref, wg_ref, gb_ref, lam_ref, h0_ref,
                y_ref, hfin_ref, af_s, bf_s, ab_s, bb_s):
    t_len, tc = x_ref.shape
    xc = _dwconv(x_ref[...], cw_ref, cb_ref)
    for n in range(tc // LANES):
        sl = slice(n * LANES, (n + 1) * LANES)
        xn = xc[:, sl]
        gates = _dot(xn.astype(BF16), wg_ref[n])
        for d, (a_s, b_s) in enumerate(((af_s, bf_s), (ab_s, bb_s))):
            r = jax.nn.sigmoid(gates[:, (2 * d) * LANES:(2 * d + 1) * LANES] + gb_ref[2 * d:2 * d + 1, sl])
            i = jax.nn.sigmoid(gates[:, (2 * d + 1) * LANES:(2 * d + 2) * LANES]
                               + gb_ref[2 * d + 1:2 * d + 2, sl])
            log_a = -LRU_C * r * _softplus(-lam_ref[d:d + 1, sl])
            a_s[:, sl] = jnp.exp(log_a)
            th = jnp.tanh(log_a)
            b_s[:, sl] = jnp.sqrt(-2.0 * th / (1.0 - th)) * (i * xn)

    def step(s, carry):
        hf, hb = carry
        hf = af_s[pl.ds(s, 1), :] * hf + bf_s[pl.ds(s, 1), :]
        bf_s[pl.ds(s, 1), :] = hf
        u = t_len - 1 - s
        hb = ab_s[pl.ds(u, 1), :] * hb + bb_s[pl.ds(u, 1), :]
        bb_s[pl.ds(u, 1), :] = hb
        return hf, hb

    hf, hb = lax.fori_loop(0, t_len, step, (h0_ref[0:1, :], h0_ref[1:2, :]), unroll=8)
    hfin_ref[0:1, :] = hf
    hfin_ref[1:2, :] = hb
    y_ref[...] = ((bf_s[...] + bb_s[...]) * jax.nn.gelu(g_ref[...], approximate=True)).astype(y_ref.dtype)


def _lru_group(proj, row_off, nb, t_len, h0, cw, cb, wg, gb, lam):
    w = cw.shape[1]
    tc = max(LANES, min(w, LRU_SCRATCH_BYTES // (4 * t_len)))
    rb = row_off // t_len
    nct = w // tc
    return pl.pallas_call(
        _lru_kernel,
        out_shape=(SDS((nb * t_len, w), BF16), SDS((nb, 2, w), F32)),
        grid=(nb, nct),
        in_specs=[
            pl.BlockSpec((t_len, tc), lambda b, j: (rb + b, j)),
            pl.BlockSpec((t_len, tc), lambda b, j: (rb + b, nct + j)),
            pl.BlockSpec((4, tc), lambda b, j: (0, j)),
            pl.BlockSpec((1, tc), lambda b, j: (0, j)),
            pl.BlockSpec((tc // LANES, LANES, 4 * LANES), lambda b, j: (j, 0, 0)),
            pl.BlockSpec((4, tc), lambda b, j: (0, j)),
            pl.BlockSpec((2, tc), lambda b, j: (0, j)),
            pl.BlockSpec((None, 2, tc), lambda b, j: (b, 0, j)),
        ],
        out_specs=(
            pl.BlockSpec((t_len, tc), lambda b, j: (b, j)),
            pl.BlockSpec((None, 2, tc), lambda b, j: (b, 0, j)),
        ),
        scratch_shapes=[pltpu.VMEM((t_len, tc), F32)] * 4,
        compiler_params=_params("arbitrary", "arbitrary"),
        name="rglru",
    )(proj, proj, cw, cb, wg, gb, lam, h0)


def _ssd_kernel(xs_ref, bm_ref, cm_ref, z_ref, dt_ref, dtt_ref,
                cwx_ref, cwb_ref, cwc_ref, cbx_ref, cbb_ref, cbc_ref,
                dtb_ref, dtbt_ref, alog_ref, alogt_ref, dsum_ref, ng_ref, s0_ref,
                y_ref, sfin_ref, xs_s, bm_s, cm_s, xd_s, dt_s, yacc_s, st_s):
    t_len, gw = xs_ref.shape
    q = SSD_CHUNK
    nc = t_len // q
    hg = dtt_ref.shape[1]
    hp = gw // hg
    per_blk = LANES // hp

    xs_s[...] = _silu(_dwconv(xs_ref[...], cwx_ref, cbx_ref))
    bm_s[...] = _silu(_dwconv(bm_ref[...], cwb_ref, cbb_ref)).astype(BF16)
    cm_s[...] = _silu(_dwconv(cm_ref[...], cwc_ref, cbc_ref)).astype(BF16)

    row = lax.broadcasted_iota(I32, (q, q), 0)
    col = lax.broadcasted_iota(I32, (q, q), 1)
    e_h = lax.broadcasted_iota(I32, (LANES, gw), 0)
    e_c = lax.broadcasted_iota(I32, (LANES, gw), 1)
    expand = (e_c // hp == e_h).astype(BF16)
    lane_in_blk = lax.broadcasted_iota(I32, (q, LANES), 1) // hp

    for d in range(2):
        causal = (row >= col) if d == 0 else (row <= col)
        tri = causal.astype(BF16)
        tri_t = ((col >= row) if d == 0 else (col <= row)).astype(F32)
        last = q - 1 if d == 0 else 0
        a_row = -jnp.exp(alog_ref[d])
        a_col = -jnp.exp(alogt_ref[d])
        dt_all = _softplus(dt_ref[...] + dtb_ref[d])
        dt_s[...] = dt_all
        xd_s[...] = xs_s[...] * _dot_sel_right(dt_all, expand)
        st_s[...] = s0_ref[d].T

        def chunk(ci, carry):
            c = ci if d == 0 else nc - 1 - ci
            c0 = pl.multiple_of(c * q, q)
            rows = pl.ds(c0, q)
            da = dt_s[rows, :] * a_row
            da_t = _softplus(dtt_ref[c] + dtbt_ref[d]) * a_col
            cum = _dot_sel_left(tri, da)
            cum_t = _dot_exact(da_t, tri_t)
            tot = cum[last:last + 1, :]
            expcum_x = _dot_sel_right(jnp.exp(cum), expand)
            decay_x = _dot_sel_right(jnp.exp(tot - cum), expand)
            tot_x = expcum_x[last:last + 1, :]
            bm = bm_s[rows, :]
            cm = cm_s[rows, :]
            xd = xd_s[rows, :]
            scores = lax.dot_general(cm, bm, (((1,), (1,)), ((), ())), preferred_element_type=F32)
            st = st_s[...]
            y_off = _dot(cm, st.astype(BF16)) * expcum_x
            for blk in range(gw // LANES):
                xd_blk = xd[:, blk * LANES:(blk + 1) * LANES]
                m_parts, x_parts = [], []
                for sub in range(per_blk):
                    h = blk * per_blk + sub
                    diff = cum[:, h:h + 1] - cum_t[h:h + 1, :]
                    lmat = jnp.exp(jnp.where(causal, diff, NEG_BIG))
                    m_parts.append((scores * lmat).astype(BF16))
                    x_parts.append(jnp.where(lane_in_blk == sub, xd_blk, 0.0).astype(BF16))
                y_diag = _dot(jnp.concatenate(m_parts, axis=1), jnp.concatenate(x_parts, axis=0))
                sl = slice(blk * LANES, (blk + 1) * LANES)
                contrib = y_diag + y_off[:, sl]
                if d == 0:
                    yacc_s[rows, sl] = contrib
                else:
                    yacc_s[rows, sl] = yacc_s[rows, sl] + contrib
            upd = lax.dot_general(bm, (xd * decay_x).astype(BF16), (((0,), (0,)), ((), ())),
                                  preferred_element_type=F32)
            st_s[...] = st * tot_x + upd
            return carry

        lax.fori_loop(0, nc, chunk, 0)
        sfin_ref[d] = st_s[...].T

    z = z_ref[...]
    y = (yacc_s[...] + dsum_ref[...] * xs_s[...]) * _silu(z)
    ms = jnp.mean(y * y, axis=-1, keepdims=True)
    y_ref[...] = (y * lax.rsqrt(ms + EPS) * ng_ref[...]).astype(y_ref.dtype)


def _ssd_group(proj, dt_pad, dtgt, row_off, nb, t_len, s0, cw, cb, dtb, dtbt, alog, alogt, dsum, ng, dims):
    off_z, off_x, off_b, off_c, inner, groups, nstate = dims
    gw = inner // groups
    hg = dtgt.shape[2]
    q = SSD_CHUNK
    nc = t_len // q
    rb = row_off // t_len
    cbk, cck = (off_b - off_x) // nstate, (off_c - off_x) // nstate
    return pl.pallas_call(
        _ssd_kernel,
        out_shape=(SDS((nb * t_len, inner), BF16), SDS((nb, 2, groups, gw, nstate), F32)),
        grid=(nb, groups),
        in_specs=[
            pl.BlockSpec((t_len, gw), lambda b, g: (rb + b, off_x // gw + g)),
            pl.BlockSpec((t_len, nstate), lambda b, g: (rb + b, off_b // nstate + g)),
            pl.BlockSpec((t_len, nstate), lambda b, g: (rb + b, off_c // nstate + g)),
            pl.BlockSpec((t_len, gw), lambda b, g: (rb + b, off_z // gw + g)),
            pl.BlockSpec((t_len, LANES), lambda b, g: (rb + b, g)),
            pl.BlockSpec((None, nc, hg, q), lambda b, g: (g, rb + b, 0, 0)),
            pl.BlockSpec((4, gw), lambda b, g: (0, g)),
            pl.BlockSpec((4, nstate), lambda b, g: (0, cbk + g)),
            pl.BlockSpec((4, nstate), lambda b, g: (0, cck + g)),
            pl.BlockSpec((1, gw), lambda b, g: (0, g)),
            pl.BlockSpec((1, nstate), lambda b, g: (0, cbk + g)),
            pl.BlockSpec((1, nstate), lambda b, g: (0, cck + g)),
            pl.BlockSpec((None, 2, 1, LANES), lambda b, g: (g, 0, 0, 0)),
            pl.BlockSpec((None, 2, hg, 1), lambda b, g: (g, 0, 0, 0)),
            pl.BlockSpec((None, 2, 1, LANES), lambda b, g: (g, 0, 0, 0)),
            pl.BlockSpec((None, 2, hg, 1), lambda b, g: (g, 0, 0, 0)),
            pl.BlockSpec((None, 1, gw), lambda b, g: (g, 0, 0)),
            pl.BlockSpec((1, gw), lambda b, g: (0, g)),
            pl.BlockSpec((None, 2, None, gw, nstate), lambda b, g: (b, 0, g, 0, 0)),
        ],
        out_specs=(
            pl.BlockSpec((t_len, gw), lambda b, g: (b, g)),
            pl.BlockSpec((None, 2, None, gw, nstate), lambda b, g: (b, 0, g, 0, 0)),
        ),
        scratch_shapes=[
            pltpu.VMEM((t_len, gw), F32),
            pltpu.VMEM((t_len, nstate), BF16),
            pltpu.VMEM((t_len, nstate), BF16),
            pltpu.VMEM((t_len, gw), F32),
            pltpu.VMEM((t_len, LANES), F32),
            pltpu.VMEM((t_len, gw), F32),
            pltpu.VMEM((nstate, gw), F32),
        ],
        compiler_params=_params("arbitrary", "arbitrary"),
        name="ssd",
    )(proj, proj, proj, proj, dt_pad, dtgt, cw, cw, cw, cb, cb, cb, dtb, dtbt, alog, alogt, dsum, ng, s0)


def _router_kernel(x_ref, g_ref, sc_ref, sh_ref, rw_ref, rb_ref,
                   h_ref, w_ref, i_ref, r_ref, cnt_ref, carry_s, *, n_experts):
    tm = x_ref.shape[0]

    @pl.when(pl.program_id(0) == 0)
    def _():
        carry_s[...] = jnp.zeros_like(carry_s)

    h = _norm_mod(x_ref[...], g_ref[...], sc_ref[...], sh_ref[...])
    h_ref[...] = _pack_halves(h)
    rw = rw_ref[...]
    rw_hi = rw.astype(BF16)
    rw_lo = (rw - rw_hi.astype(F32)).astype(BF16)
    h_hi = h.astype(BF16)
    h_lo = (h - h_hi.astype(F32)).astype(BF16)
    logits = _dot(h_hi, rw_hi) + _dot(h_lo, rw_hi) + _dot(h_hi, rw_lo) + rb_ref[...]
    lane = lax.broadcasted_iota(I32, (tm, LANES), 1)
    cur = jnp.where(lane < n_experts, logits, -jnp.inf)
    vals, idxs = [], []
    member = jnp.zeros((tm, LANES), F32)
    for _k in range(TOP_K):
        m = jnp.max(cur, axis=-1, keepdims=True)
        idx = jnp.min(jnp.where(cur == m, lane, LANES), axis=-1, keepdims=True)
        sel = lane == idx
        vals.append(m)
        idxs.append(idx)
        member = member + sel.astype(F32)
        cur = jnp.where(sel, -jnp.inf, cur)
    exps = [jnp.exp(v - vals[0]) for v in vals]
    denom = exps[0]
    for e in exps[1:]:
        denom = denom + e
    r_i = lax.broadcasted_iota(I32, (tm, tm), 0)
    c_i = lax.broadcasted_iota(I32, (tm, tm), 1)
    earlier = (c_i < r_i).astype(BF16)
    before = _dot(earlier, member.astype(BF16)) + carry_s[...]
    w_out = jnp.zeros((tm, LANES), F32)
    i_out = jnp.zeros((tm, LANES), I32)
    r_out = jnp.zeros((tm, LANES), I32)
    for k in range(TOP_K):
        rank = jnp.sum(jnp.where(lane == idxs[k], before, 0.0), axis=-1, keepdims=True)
        w_out = jnp.where(lane == k, exps[k] / denom, w_out)
        i_out = jnp.where(lane == k, idxs[k], i_out)
        r_out = jnp.where(lane == k, rank.astype(I32), r_out)
    w_ref[...] = w_out
    i_ref[...] = i_out
    r_ref[...] = r_out
    carry_s[...] = carry_s[...] + jnp.sum(member, axis=0, keepdims=True)
    cnt_ref[...] = jnp.broadcast_to(carry_s[...], cnt_ref.shape)


def _router(x, g, modr, sh_idx, sc_idx, cond_of_tile, rw_pad, rb_pad, n_experts, tm):
    n, d = x.shape
    return pl.pallas_call(
        functools.partial(_router_kernel, n_experts=n_experts),
        out_shape=(SDS((n, d // 2), U32), SDS((n, LANES), F32), SDS((n, LANES), I32),
                   SDS((n, LANES), I32), SDS((8, LANES), F32)),
        grid=(n // tm,),
        in_specs=[
            pl.BlockSpec((tm, d), lambda i: (i, 0)),
            pl.BlockSpec((1, d), lambda i: (0, 0)),
            pl.BlockSpec((None, None, 1, d), lambda i: (cond_of_tile(i, tm), sc_idx, 0, 0)),
            pl.BlockSpec((None, None, 1, d), lambda i: (cond_of_tile(i, tm), sh_idx, 0, 0)),
            pl.BlockSpec((d, LANES), lambda i: (0, 0)),
            pl.BlockSpec((1, LANES), lambda i: (0, 0)),
        ],
        out_specs=(
            pl.BlockSpec((tm, d // 2), lambda i: (i, 0)),
            pl.BlockSpec((tm, LANES), lambda i: (i, 0)),
            pl.BlockSpec((tm, LANES), lambda i: (i, 0)),
            pl.BlockSpec((tm, LANES), lambda i: (i, 0)),
            pl.BlockSpec((8, LANES), lambda i: (0, 0)),
        ),
        scratch_shapes=[pltpu.VMEM((1, LANES), F32)],
        compiler_params=_params("arbitrary"),
        name="router",
    )(x, g.reshape(1, d), modr, modr, rw_pad, rb_pad)


def _row_copy(src_hbm, dst, sem, src_row, dst_row):
    return pltpu.make_async_copy(src_hbm.at[pl.ds(src_row, 1), :], dst.at[pl.ds(dst_row, 1), :], sem)


def _dispatch_kernel(src_ref, nv_ref, h_hbm, o_ref, buf, sem):
    tm = o_ref.shape[0]
    i = pl.program_id(0)
    nv = nv_ref[0]

    def issue(tile, slot):
        base = tile * tm

        def body(q, c):
            for p in range(2):
                r = 2 * q + p
                _row_copy(h_hbm, buf.at[slot], sem.at[slot], src_ref[base + r], r).start(priority=p)
            return c

        lax.fori_loop(0, tm // 2, body, 0, unroll=4)

    def drain(slot):
        pltpu.make_async_copy(h_hbm.at[pl.ds(0, tm), :], buf.at[slot], sem.at[slot]).wait()

    @pl.when(jnp.logical_and(i == 0, nv > 0))
    def _():
        issue(0, 0)

    @pl.when(i + 1 < nv)
    def _():
        issue(i + 1, (i + 1) & 1)

    @pl.when(i < nv)
    def _():
        slot = i & 1
        drain(slot)
        half = buf.shape[2]
        lo, hi = _unpack_halves(buf[slot])
        o_ref[:, :half] = lo.astype(o_ref.dtype)
        o_ref[:, half:] = hi.astype(o_ref.dtype)

    @pl.when(i >= nv)
    def _():
        o_ref[...] = jnp.zeros_like(o_ref)


def _dispatch(h_packed, src_tok, n_valid, n_tiles, tm):
    half = h_packed.shape[1]
    d = 2 * half
    return pl.pallas_call(
        _dispatch_kernel,
        out_shape=SDS((n_tiles * tm, d), BF16),
        grid_spec=pltpu.PrefetchScalarGridSpec(
            num_scalar_prefetch=2,
            grid=(n_tiles,),
            in_specs=[pl.BlockSpec(memory_space=pl.ANY)],
            out_specs=pl.BlockSpec((tm, d), lambda i, s, nv: (i, 0)),
            scratch_shapes=[pltpu.VMEM((2, tm, half), U32), pltpu.SemaphoreType.DMA((2,))],
        ),
        compiler_params=_params("arbitrary"),
        name="moe_dispatch",
    )(src_tok, n_valid, h_packed)


def _weight_copy(w_hbm, buf, sem, layer, expert, col_blk, slot, tn):
    cols = pl.ds(pl.multiple_of(col_blk * tn, tn), tn)
    return pltpu.make_async_copy(w_hbm.at[layer, expert, :, cols], buf.at[slot], sem.at[slot])


def _group_weights(te_ref, first_ref, nxt_ref, gidx_ref, meta_ref, weights, layer, tn):
    j = pl.program_id(0)
    i = pl.program_id(1)
    nj = pl.num_programs(0)
    ng = meta_ref[1]

    @pl.when(first_ref[i] == 1)
    def _():
        slot = (gidx_ref[i] + j * ng) & 1

        @pl.when(jnp.logical_and(i == 0, j == 0))
        def _():
            for w_hbm, buf, _bf, sem in weights:
                _weight_copy(w_hbm, buf, sem, layer, te_ref[i], j, slot, tn).start()

        for w_hbm, buf, _bf, sem in weights:
            _weight_copy(w_hbm, buf, sem, layer, te_ref[i], j, slot, tn).wait()

        next_j = jnp.where(gidx_ref[i] == ng - 1, j + 1, j)

        @pl.when(next_j < nj)
        def _():
            for w_hbm, buf, _bf, sem in weights:
                _weight_copy(w_hbm, buf, sem, layer, nxt_ref[i], next_j, 1 - slot, tn).start()

        for _w, buf, bf, _sem in weights:
            k = bf.shape[0]
            rows_per = min(CAST_ROWS, k)

            def cast(c, carry, buf=buf, bf=bf, rows_per=rows_per):
                rows = pl.ds(pl.multiple_of(c * rows_per, rows_per), rows_per)
                bf[rows, :] = buf[slot, rows, :].astype(BF16)
                return carry

            lax.fori_loop(0, k // rows_per, cast, 0)


def _gmm_up_kernel(te_ref, first_ref, nxt_ref, gidx_ref, tb_ref, meta_ref,
                   x_ref, wg_hbm, wu_hbm, bg_ref, bu_ref, o_ref,
                   wg_buf, wu_buf, wg_bf, wu_bf, sem_g, sem_u, *, layer):
    i = pl.program_id(1)
    tn = o_ref.shape[1]

    @pl.when(i < meta_ref[0])
    def _():
        _group_weights(te_ref, first_ref, nxt_ref, gidx_ref, meta_ref,
                       [(wg_hbm, wg_buf, wg_bf, sem_g), (wu_hbm, wu_buf, wu_bf, sem_u)], layer, tn)
        x = x_ref[...]
        g = jnp.minimum(_dot(x, wg_bf[...]) + bg_ref[...], SWIGLU_LIMIT)
        u = jnp.clip(_dot(x, wu_bf[...]) + bu_ref[...], -SWIGLU_LIMIT, SWIGLU_LIMIT)
        o_ref[...] = ((u + 1.0) * g * jax.nn.sigmoid(SWIGLU_ALPHA * g)).astype(o_ref.dtype)

    @pl.when(i >= meta_ref[0])
    def _():
        o_ref[...] = jnp.zeros_like(o_ref)


def _gmm_up(xs, wg, wu, bg, bu, plan, layer, tm, tn):
    ns, d = xs.shape
    depth, n_exp, _, de = wg.shape
    n_tiles = ns // tm
    eoff = layer * n_exp
    return pl.pallas_call(
        functools.partial(_gmm_up_kernel, layer=layer),
        out_shape=SDS((ns, de), BF16),
        grid_spec=pltpu.PrefetchScalarGridSpec(
            num_scalar_prefetch=6,
            grid=(de // tn, n_tiles),
            in_specs=[
                pl.BlockSpec((tm, d), lambda j, i, te, fi, nx, gi, tb, mt: (tb[i], 0)),
                pl.BlockSpec(memory_space=pl.ANY),
                pl.BlockSpec(memory_space=pl.ANY),
                pl.BlockSpec((None, 1, tn), lambda j, i, te, fi, nx, gi, tb, mt: (eoff + te[i], 0, j)),
                pl.BlockSpec((None, 1, tn), lambda j, i, te, fi, nx, gi, tb, mt: (eoff + te[i], 0, j)),
            ],
            out_specs=pl.BlockSpec((tm, tn), lambda j, i, te, fi, nx, gi, tb, mt: (i, j)),
            scratch_shapes=[
                pltpu.VMEM((2, d, tn), F32), pltpu.VMEM((2, d, tn), F32),
                pltpu.VMEM((d, tn), BF16), pltpu.VMEM((d, tn), BF16),
                pltpu.SemaphoreType.DMA((2,)), pltpu.SemaphoreType.DMA((2,)),
            ],
        ),
        compiler_params=_params("arbitrary", "arbitrary"),
        name="moe_up",
    )(*plan, xs, wg, wu, bg.reshape(depth * n_exp, 1, de), bu.reshape(depth * n_exp, 1, de))


def _gmm_down_kernel(te_ref, first_ref, nxt_ref, gidx_ref, tb_ref, meta_ref,
                     a_ref, w_hbm, b_ref, o_ref, w_buf, w_bf, sem, *, layer):
    i = pl.program_id(1)
    tn = w_bf.shape[1]

    @pl.when(i < meta_ref[0])
    def _():
        _group_weights(te_ref, first_ref, nxt_ref, gidx_ref, meta_ref, [(w_hbm, w_buf, w_bf, sem)], layer, tn)
        o_ref[...] = _pack_halves(_dot(a_ref[...], w_bf[...]) + b_ref[...])

    @pl.when(i >= meta_ref[0])
    def _():
        o_ref[...] = jnp.zeros_like(o_ref)


def _gmm_down(act, wd, bd, plan, layer, tm, tn):
    ns, de = act.shape
    depth, n_exp, _, d = wd.shape
    n_tiles = ns // tm
    eoff = layer * n_exp
    return pl.pallas_call(
        functools.partial(_gmm_down_kernel, layer=layer),
        out_shape=SDS((ns, d // 2), U32),
        grid_spec=pltpu.PrefetchScalarGridSpec(
            num_scalar_prefetch=6,
            grid=(d // tn, n_tiles),
            in_specs=[
                pl.BlockSpec((tm, de), lambda j, i, te, fi, nx, gi, tb, mt: (tb[i], 0)),
                pl.BlockSpec(memory_space=pl.ANY),
                pl.BlockSpec((None, 1, tn), lambda j, i, te, fi, nx, gi, tb, mt: (eoff + te[i], 0, j)),
            ],
            out_specs=pl.BlockSpec((tm, tn // 2), lambda j, i, te, fi, nx, gi, tb, mt: (i, j)),
            scratch_shapes=[
                pltpu.VMEM((2, de, tn), F32), pltpu.VMEM((de, tn), BF16), pltpu.SemaphoreType.DMA((2,)),
            ],
        ),
        compiler_params=_params("arbitrary", "arbitrary"),
        name="moe_down",
    )(*plan, act, wd, bd.reshape(depth * n_exp, 1, d))


def _combine_kernel(pos_ref, y_hbm, x_ref, w_ref, g_ref, fg_ref, *rest, final_norm, n_first, col_blk):
    outs, (buf, xs_s, sem) = rest[:-3], rest[-3:]
    tm = x_ref.shape[0]
    i = pl.program_id(0)
    n_steps = pl.num_programs(0)

    def issue(tile, slot):
        base = tile * tm * TOP_K

        def body(r, c):
            for k in range(TOP_K):
                _row_copy(y_hbm, buf.at[slot, k], sem.at[slot], pos_ref[base + r * TOP_K + k], r).start()
            return c

        lax.fori_loop(0, tm, body, 0, unroll=4)

    def drain(slot):
        for k in range(TOP_K):
            pltpu.make_async_copy(y_hbm.at[pl.ds(0, tm), :], buf.at[slot, k], sem.at[slot]).wait()

    @pl.when(i == 0)
    def _():
        issue(0, 0)

    @pl.when(i + 1 < n_steps)
    def _():
        issue(i + 1, (i + 1) & 1)

    slot = i & 1
    drain(slot)
    w = w_ref[...]
    half_blk = col_blk // 2
    sub = min(half_blk, 2 * LANES)
    dst = xs_s if final_norm else outs[0]
    for jb in range(x_ref.shape[1] // col_blk):
        for s0 in range(0, half_blk, sub):
            cols = slice(jb * half_blk + s0, jb * half_blk + s0 + sub)
            acc_lo = acc_hi = None
            for k in range(TOP_K):
                lo, hi = _unpack_halves(buf[slot, k, :, cols])
                wk = w[:, k:k + 1]
                acc_lo = wk * lo if acc_lo is None else acc_lo + wk * lo
                acc_hi = wk * hi if acc_hi is None else acc_hi + wk * hi
            for part, acc in ((0, acc_lo), (half_blk, acc_hi)):
                oc = slice(jb * col_blk + part + s0, jb * col_blk + part + s0 + sub)
                dst[:, oc] = x_ref[:, oc] + g_ref[:, oc] * acc
    if final_norm:
        x = xs_s[...]
        ms = jnp.mean(x * x, axis=-1, keepdims=True)
        x = x * lax.rsqrt(ms + EPS) * fg_ref[...]
        first_ref, rest_ref = outs

        @pl.when(i < n_first)
        def _():
            first_ref[...] = x

        @pl.when(i >= n_first)
        def _():
            rest_ref[...] = x


def _combine(y_sorted, pos_flat, x, w_pad, modr, g_idx, cond_of_tile, final_g, final_norm, n_first_rows,
             col_blk, tm):
    n, d = x.shape
    n_first = n_first_rows // tm
    if final_norm:
        out_shape = (SDS((n_first_rows, d), F32), SDS((n - n_first_rows, d), F32))
        out_specs = (pl.BlockSpec((tm, d), lambda i, p: (jnp.minimum(i, n_first - 1), 0)),
                     pl.BlockSpec((tm, d), lambda i, p: (jnp.maximum(i - n_first, 0), 0)))
    else:
        out_shape = SDS((n, d), F32)
        out_specs = pl.BlockSpec((tm, d), lambda i, p: (i, 0))
    return pl.pallas_call(
        functools.partial(_combine_kernel, final_norm=final_norm, n_first=n_first, col_blk=col_blk),
        out_shape=out_shape,
        grid_spec=pltpu.PrefetchScalarGridSpec(
            num_scalar_prefetch=1,
            grid=(n // tm,),
            in_specs=[
                pl.BlockSpec(memory_space=pl.ANY),
                pl.BlockSpec((tm, d), lambda i, p: (i, 0)),
                pl.BlockSpec((tm, LANES), lambda i, p: (i, 0)),
                pl.BlockSpec((None, None, 1, d), lambda i, p: (cond_of_tile(i, tm), g_idx, 0, 0)),
                pl.BlockSpec((1, d), lambda i, p: (0, 0)),
            ],
            out_specs=out_specs,
            scratch_shapes=[pltpu.VMEM((2, TOP_K, tm, d // 2), U32), pltpu.VMEM((tm, d), F32),
                            pltpu.SemaphoreType.DMA((2,))],
        ),
        compiler_params=_params("arbitrary"),
        name="moe_combine",
    )(pos_flat, y_sorted, x, w_pad, modr, final_g.reshape(1, d))


def _pick_tile(n, pref):
    t = min(pref, n)
    while n % t:
        t //= 2
    return t


def _routing_plan(counts, n_tiles, tm):
    n_experts = counts.shape[0]
    padded = (counts + tm - 1) // tm * tm
    ends = jnp.cumsum(padded)
    starts = ends - padded
    n_valid = ends[-1] // tm
    tile_ids = jnp.arange(n_tiles, dtype=I32)
    tile_b = jnp.minimum(tile_ids, jnp.maximum(n_valid - 1, 0))
    tile_e = jnp.minimum(jnp.sum(ends[None, :] <= (tile_b * tm)[:, None], axis=1), n_experts - 1).astype(I32)
    prev_e = jnp.concatenate([jnp.full((1,), -1, I32), tile_e[:-1]])
    first = ((tile_e != prev_e) & (tile_ids < n_valid)).astype(I32)
    nonempty = padded > 0
    gidx_of_e = jnp.cumsum(nonempty.astype(I32)) - 1
    n_groups = jnp.sum(nonempty.astype(I32))
    big = n_experts
    idx = jnp.where(nonempty, jnp.arange(n_experts, dtype=I32), big)
    at_or_after = lax.cummin(idx, reverse=True)
    after = jnp.concatenate([at_or_after[1:], jnp.full((1,), big, I32)])
    nxt_of_e = jnp.where(after == big, at_or_after[0], after)
    nxt_of_e = jnp.minimum(nxt_of_e, n_experts - 1).astype(I32)
    meta = jnp.stack([n_valid, n_groups]).astype(I32)
    plan = (tile_e, first, nxt_of_e[tile_e], gidx_of_e[tile_e].astype(I32), tile_b.astype(I32), meta)
    return starts, plan


def kernel(x_prompt, x_sample, c, state_lru, state_ssd, c_ctx, ada_w, ada_b, norm1_g, norm2_g, w_in, lru_conv_w, lru_conv_b, lru_wa, lru_ba, lru_wi, lru_bi, lru_lambda, ssd_conv_w, ssd_conv_b, ssd_dt_bias, ssd_a_log, ssd_d, ssd_norm_g, w_out, router_w, router_b, exp_w_gate, exp_b_gate, exp_w_up, exp_b_up, exp_w_down, exp_b_down, final_norm_g):
    b_ctx, t_ctx, d = x_prompt.shape
    b_lat, t_lat, _ = x_sample.shape
    depth = ada_w.shape[0]
    n_ctx, n_lat = b_ctx * t_ctx, b_lat * t_lat
    n = n_ctx + n_lat
    lru_w = lru_conv_w.shape[-1]
    _, _, heads, hp, nstate = state_ssd.shape[1:]
    inner = heads * hp
    conv_ch = ssd_conv_w.shape[-1]
    groups = (conv_ch - inner) // (2 * nstate)
    gw = inner // groups
    hg = heads // groups
    n_experts = router_w.shape[-1]
    lru_blk = lru_wa.shape[3]
    assert lru_blk == LANES and hg <= 8 and LANES % hp == 0 and n_experts <= LANES
    assert t_ctx % SSD_CHUNK == 0 and t_lat % SSD_CHUNK == 0 and n_ctx % t_lat == 0
    assert b_lat + 1 <= 8

    off_z = 2 * lru_w
    off_x = off_z + inner
    off_b = off_x + inner
    off_c = off_b + groups * nstate
    n_main = off_c + groups * nstate
    ssd_dims = (off_z, off_x, off_b, off_c, inner, groups, nstate)

    tm = _pick_tile(min(t_ctx, t_lat), 512)
    cond_of_tile = functools.partial(_cond_row, n_ctx=n_ctx, t_lat=t_lat)

    xa, xb = x_prompt.reshape(n_ctx, d), x_sample.reshape(n_lat, d)
    cond8 = jnp.zeros((8, d), F32).at[0].set(c_ctx).at[1:1 + b_lat].set(c)
    mod = _ada_mod(cond8, ada_w, ada_b)
    modr_all = mod[:, :1 + b_lat].reshape(depth, 1 + b_lat, 6, 1, d)

    tn_in = _pick_tile(n_main, 1024)
    tn_out = _pick_tile(d, 1024)
    tm_mm = _pick_tile(n, 1024)
    tm_out = _pick_tile(min(n_ctx, t_lat), 512)
    tn_down = _pick_tile(d, MOE_DOWN_COLS)
    moe_tm = MOE_TILE
    n_tiles = (n * TOP_K + n_experts * (moe_tm - 1)) // moe_tm + 1
    tok_ids = jnp.repeat(jnp.arange(n, dtype=I32), TOP_K)
    rw_pad = jnp.zeros((depth, d, LANES), F32).at[:, :, :n_experts].set(router_w)
    rb_pad = jnp.zeros((depth, 1, LANES), F32).at[:, 0, :n_experts].set(router_b)

    lru_states, ssd_states = [], []
    zeros_lru = jnp.zeros((b_ctx, 2, lru_w), F32)
    zeros_ssd = jnp.zeros((b_ctx, 2, groups, gw, nstate), F32)
    y_final = None
    for l in range(depth):
        modr = modr_all[l]
        h = _normmod(xa, xb, norm1_g[l], modr, 0, 1, cond_of_tile, tm)
        w_in_bf = w_in[l].astype(BF16)
        proj = _matmul(h, w_in_bf[:, :n_main], tm_mm, tn_in)
        w_dt = jnp.zeros((d, groups, LANES), BF16).at[:, :, :hg].set(
            w_in_bf[:, n_main:].reshape(d, groups, hg)).reshape(d, groups * LANES)
        dt_pad = _matmul(h, w_dt, tm_mm, _pick_tile(groups * LANES, 512))
        dtgt = dt_pad.reshape(n // SSD_CHUNK, SSD_CHUNK, groups, LANES)[..., :hg].transpose(2, 0, 3, 1)

        wg = jnp.concatenate([lru_wa[l, 0], lru_wi[l, 0], lru_wa[l, 1], lru_wi[l, 1]], axis=-1).astype(BF16)
        gb = jnp.stack([lru_ba[l, 0], lru_bi[l, 0], lru_ba[l, 1], lru_bi[l, 1]], axis=0)
        lru_args = (lru_conv_w[l], lru_conv_b[l].reshape(1, lru_w), wg, gb, lru_lambda[l])
        y_lru_c, h_c = _lru_group(proj, 0, b_ctx, t_ctx, zeros_lru, *lru_args)
        y_lru_s, _ = _lru_group(proj, n_ctx, b_lat, t_lat, state_lru[:, l], *lru_args)

        def per_group(p, fill):
            pg = p.reshape(2, groups, hg).transpose(1, 0, 2)
            rowf = jnp.full((groups, 2, 1, LANES), fill, F32).at[:, :, 0, :hg].set(pg)
            return rowf, pg.reshape(groups, 2, hg, 1)

        dtb, dtbt = per_group(ssd_dt_bias[l], 0.0)
        alog, alogt = per_group(ssd_a_log[l], 0.0)
        dsum = jnp.repeat((ssd_d[l, 0] + ssd_d[l, 1]).reshape(groups, 1, hg), hp, axis=-1)
        ssd_args = (ssd_conv_w[l], ssd_conv_b[l].reshape(1, conv_ch), dtb, dtbt, alog, alogt, dsum,
                    ssd_norm_g[l].reshape(1, inner), ssd_dims)
        y_ssd_c, s_c = _ssd_group(proj, dt_pad, dtgt, 0, b_ctx, t_ctx, zeros_ssd, *ssd_args)
        s0_lat = state_ssd[:, l].reshape(b_lat, 2, groups, gw, nstate)
        y_ssd_s, _ = _ssd_group(proj, dt_pad, dtgt, n_ctx, b_lat, t_lat, s0_lat, *ssd_args)
        lru_states.append(h_c)
        ssd_states.append(s_c.reshape(b_ctx, 2, heads, hp, nstate))

        y_lru = jnp.concatenate([y_lru_c, y_lru_s], axis=0)
        y_ssd = jnp.concatenate([y_ssd_c, y_ssd_s], axis=0)
        x = _wout(y_lru, y_ssd, w_out[l].astype(BF16), xa, xb, modr, 2, cond_of_tile, tm_out, tn_out)

        rt = _pick_tile(tm, 256)
        h2, w_pad, idx_pad, rank_pad, cnt = _router(x, norm2_g[l], modr, 3, 4, cond_of_tile,
                                                     rw_pad[l], rb_pad[l], n_experts, rt)
        starts, plan = _routing_plan(cnt[0, :n_experts].astype(I32), n_tiles, moe_tm)
        pos = (starts[idx_pad[:, :TOP_K]] + rank_pad[:, :TOP_K]).reshape(-1)
        src_tok = jnp.zeros((n_tiles * moe_tm,), I32).at[pos].set(tok_ids)
        n_valid = plan[-1][:1]

        xs = _dispatch(h2, src_tok, n_valid, n_tiles, moe_tm)
        act = _gmm_up(xs, exp_w_gate, exp_w_up, exp_b_gate, exp_b_up, plan, l, moe_tm,
                      _pick_tile(exp_w_gate.shape[-1], MOE_UP_COLS))
        y_sorted = _gmm_down(act, exp_w_down, exp_b_down, plan, l, moe_tm, tn_down)
        last = l == depth - 1
        out = _combine(y_sorted, pos, x, w_pad, modr, 5, cond_of_tile, final_norm_g, last, n_ctx,
                       tn_down, _pick_tile(tm, COMBINE_TILE))
        if last:
            y_final = out
        else:
            xa = xb = out

    y_prompt = y_final[0].reshape(b_ctx, t_ctx, d)
    y_sample = y_final[1].reshape(b_lat, t_lat, d)
    return (y_prompt, y_sample, jnp.stack(lru_states, axis=1), jnp.stack(ssd_states, axis=1))
```
